```python
import jax
import jax.numpy as jnp
from jax import lax
import numpy as np

D_MODEL = 2048
BATCH = 1
SEQ = 8192
DEPTH = 2

HGRN_HEADS = 8
HGRN_HEAD_K = 128
HGRN_HEAD_V = D_MODEL // 2 // HGRN_HEADS
HGRN_KEY = HGRN_HEADS * HGRN_HEAD_K
HGRN_VAL = HGRN_HEADS * HGRN_HEAD_V
GLA_HEADS = 4
GLA_VAL = D_MODEL // 2
GLA_KEY = GLA_VAL // 2
GLA_HEAD_K = GLA_KEY // GLA_HEADS
GLA_HEAD_V = GLA_VAL // GLA_HEADS
GLA_GATE_RANK = 16
GLA_GATE_NORMALIZER = 16.0
CHUNK = 64
D_FF_DENSE = 5632
N_EXPERTS = 8
TOP_K = 2
D_FF_EXPERT = 7168
N_DENSE = (DEPTH + 1) // 2
N_MOE = DEPTH // 2
EPS = 1e-6

IN_SPLITS = (HGRN_KEY, HGRN_KEY, HGRN_VAL, HGRN_VAL, GLA_KEY, GLA_KEY, GLA_VAL, GLA_VAL, GLA_GATE_RANK, D_MODEL, D_MODEL)
D_IN = sum(IN_SPLITS)

kernel_name = "hybrid_hgrn2_gla_moe_block"


def rms_norm(x, w):
    x32 = x.astype(jnp.float32)
    y = x32 * lax.rsqrt(jnp.mean(x32 * x32, axis=-1, keepdims=True) + EPS)
    return (y * w.astype(jnp.float32)).astype(x.dtype)


def split_heads(a, n_heads):
    b, t, _ = a.shape
    return a.reshape(b, t, n_heads, -1).transpose(0, 2, 1, 3)


def chunked_gated_linear_attention(q, k, v, log_f, scale):
    b, h, t, dk = q.shape
    dv = v.shape[-1]
    n_chunks = t // CHUNK

    def to_chunks(a):
        return a.reshape(b, h, n_chunks, CHUNK, a.shape[-1]).transpose(2, 0, 1, 3, 4)

    causal = jnp.tril(jnp.ones((CHUNK, CHUNK), dtype=bool))[:, :, None]

    def step(state, inp):
        qc, kc, vc, gc = inp
        qc = qc.astype(jnp.float32) * scale
        kc = kc.astype(jnp.float32)
        vc = vc.astype(jnp.float32)
        cum = jnp.cumsum(gc.astype(jnp.float32), axis=2)
        rel = cum[:, :, :, None, :] - cum[:, :, None, :, :]
        decay = jnp.exp(jnp.where(causal, rel, -jnp.inf))
        scores = jnp.einsum("bhid,bhjd,bhijd->bhij", qc, kc, decay)
        o = (jnp.einsum("bhij,bhjv->bhiv", scores, vc)
             + jnp.einsum("bhid,bhdv->bhiv", qc * jnp.exp(cum), state))
        last = cum[:, :, -1:, :]
        state = (jnp.exp(last[:, :, 0, :])[..., None] * state
                 + jnp.einsum("bhjd,bhjv->bhdv", kc * jnp.exp(last - cum), vc))
        return state, o

    state0 = jnp.zeros((b, h, dk, dv), jnp.float32)
    _, o = lax.scan(step, state0, (to_chunks(q), to_chunks(k), to_chunks(v), to_chunks(log_f)))
    return o.transpose(1, 2, 0, 3, 4).reshape(b, h, t, dv)


def gated_head_norm(o, w, gate):
    b, h, t, d = o.shape
    o = o.transpose(0, 2, 1, 3)
    o = o * lax.rsqrt(jnp.mean(o * o, axis=-1, keepdims=True) + EPS) * w.astype(jnp.float32)
    return (o.reshape(b, t, h * d) * jax.nn.silu(gate.astype(jnp.float32))).astype(gate.dtype)


def hybrid_mixer(h, layer, w_in, lower_bound_logits, gk_up_w, gk_up_b, hgrn_norm_w, gla_norm_w,
                 w_proj_a, w_proj_b, w_out):
    u = h @ w_in
    cuts = np.cumsum(IN_SPLITS)[:-1].tolist()
    a_q, a_f, a_i, a_g, b_q, b_k, b_v, b_g, b_gk, m_a, m_b = jnp.split(u, cuts, axis=-1)

    bounds = jnp.cumsum(jax.nn.softmax(lower_bound_logits.astype(jnp.float32), axis=0), axis=0)
    lb = bounds[layer] - bounds[0]
    z = a_f.astype(jnp.float32)
    log_f = jnp.logaddexp(jnp.log(lb), jnp.log1p(-lb) + jax.nn.log_sigmoid(z))
    k_a = (1.0 - lb) * jax.nn.sigmoid(-z)
    o_a = chunked_gated_linear_attention(
        split_heads(a_q, HGRN_HEADS), split_heads(k_a, HGRN_HEADS),
        split_heads(a_i, HGRN_HEADS), split_heads(log_f, HGRN_HEADS), HGRN_HEAD_K ** -0.5)
    y_a = gated_head_norm(o_a, hgrn_norm_w, a_g)

    log_gk = jax.nn.log_sigmoid((b_gk @ gk_up_w + gk_up_b).astype(jnp.float32)) / GLA_GATE_NORMALIZER
    o_b = chunked_gated_linear_attention(
        split_heads(b_q, GLA_HEADS), split_heads(b_k, GLA_HEADS),
        split_heads(b_v, GLA_HEADS), split_heads(log_gk, GLA_HEADS), GLA_HEAD_K ** -0.5)
    y_b = gated_head_norm(o_b, gla_norm_w, b_g)

    merged = jax.nn.sigmoid(m_a) * (y_a @ w_proj_a) + jax.nn.sigmoid(m_b) * (y_b @ w_proj_b)
    return merged @ w_out


def swiglu(h, w_gate, w_up, w_down):
    return (jax.nn.silu(h @ w_gate) * (h @ w_up)) @ w_down


def moe_swiglu(h, router_w, router_b, w_gate, w_up, w_down):
    logits = (h @ router_w).astype(jnp.float32) + router_b.astype(jnp.float32)
    top_logits, top_idx = lax.top_k(logits, TOP_K)
    top_w = jax.nn.softmax(top_logits, axis=-1)
    combine = jnp.sum(jax.nn.one_hot(top_idx, N_EXPERTS, dtype=jnp.float32) * top_w[..., None],
                      axis=-2).astype(h.dtype)
    out = jnp.zeros_like(h)
    for e in range(N_EXPERTS):
        out = out + combine[..., e:e + 1] * swiglu(h, w_gate[e], w_up[e], w_down[e])
    return out


def setup_inputs(seed: int = 0) -> dict:
    key = jax.random.key(seed)
    ks = jax.random.split(key, 21)
    f32 = jnp.float32

    def normal(k, shape, scale):
        return jax.random.normal(k, shape, f32) * scale

    def gain(k, shape):
        return 1.0 + 0.02 * jax.random.normal(k, shape, f32)

    return {
        "x": normal(ks[0], (BATCH, SEQ, D_MODEL), 1.0),
        "mix_norm_w": gain(ks[1], (DEPTH, D_MODEL)),
        "w_in": normal(ks[2], (DEPTH, D_MODEL, D_IN), D_MODEL ** -0.5),
        "lower_bound_logits": normal(ks[3], (DEPTH, HGRN_KEY), 1.0),
        "gk_up_w": normal(ks[4], (DEPTH, GLA_GATE_RANK, GLA_KEY), GLA_GATE_RANK ** -0.5),
        "gk_up_b": normal(ks[5], (DEPTH, GLA_KEY), 0.1),
        "hgrn_norm_w": gain(ks[6], (DEPTH, HGRN_HEAD_V)),
        "gla_norm_w": gain(ks[7], (DEPTH, GLA_HEAD_V)),
        "w_proj_a": normal(ks[8], (DEPTH, HGRN_VAL, D_MODEL), HGRN_VAL ** -0.5),
        "w_proj_b": normal(ks[9], (DEPTH, GLA_VAL, D_MODEL), GLA_VAL ** -0.5),
        "w_out": normal(ks[10], (DEPTH, D_MODEL, D_MODEL), D_MODEL ** -0.5),
        "ffn_norm_w": gain(ks[11], (DEPTH, D_MODEL)),
        "dense_w_gate": normal(ks[12], (N_DENSE, D_MODEL, D_FF_DENSE), D_MODEL ** -0.5),
        "dense_w_up": normal(ks[13], (N_DENSE, D_MODEL, D_FF_DENSE), D_MODEL ** -0.5),
        "dense_w_down": normal(ks[14], (N_DENSE, D_FF_DENSE, D_MODEL), D_FF_DENSE ** -0.5),
        "router_w": normal(ks[15], (N_MOE, D_MODEL, N_EXPERTS), D_MODEL ** -0.5),
        "router_b": normal(ks[16], (N_MOE, N_EXPERTS), 0.01),
        "moe_w_gate": normal(ks[17], (N_MOE, N_EXPERTS, D_MODEL, D_FF_EXPERT), D_MODEL ** -0.5),
        "moe_w_up": normal(ks[18], (N_MOE, N_EXPERTS, D_MODEL, D_FF_EXPERT), D_MODEL ** -0.5),
        "moe_w_down": normal(ks[19], (N_MOE, N_EXPERTS, D_FF_EXPERT, D_MODEL), D_FF_EXPERT ** -0.5),
        "final_norm_w": gain(ks[20], (D_MODEL,)),
    }


def reference(x, mix_norm_w, w_in, lower_bound_logits, gk_up_w, gk_up_b, hgrn_norm_w, gla_norm_w,
              w_proj_a, w_proj_b, w_out, ffn_norm_w, dense_w_gate, dense_w_up, dense_w_down,
              router_w, router_b, moe_w_gate, moe_w_up, moe_w_down, final_norm_w):
    for layer in range(DEPTH):
        h = rms_norm(x, mix_norm_w[layer])
        x = x + hybrid_mixer(h, layer, w_in[layer], lower_bound_logits, gk_up_w[layer], gk_up_b[layer],
                             hgrn_norm_w[layer], gla_norm_w[layer], w_proj_a[layer], w_proj_b[layer],
                             w_out[layer])
        h = rms_norm(x, ffn_norm_w[layer])
        j = layer // 2
        if layer % 2 == 0:
            x = x + swiglu(h, dense_w_gate[j], dense_w_up[j], dense_w_down[j])
        else:
            x = x + moe_swiglu(h, router_w[j], router_b[j], moe_w_gate[j], moe_w_up[j], moe_w_down[j])
    return rms_norm(x, final_norm_w)
```

```python
import functools

import numpy as np
import jax
import jax.numpy as jnp
from jax import lax
from jax.experimental import pallas as pl
from jax.experimental.pallas import tpu as pltpu

F32 = jnp.float32
BF16 = jnp.bfloat16

D_MODEL = 2048
DEPTH = 2
HGRN_HEADS = 8
HGRN_KEY = 1024
HGRN_VAL = 1024
GLA_HEADS = 4
GLA_KEY = 512
GLA_VAL = 1024
HEAD_K = 128
GLA_GATE_RANK = 16
GLA_GATE_NORMALIZER = 16.0
D_FF_DENSE = 5632
N_EXPERTS = 8
TOP_K = 2
D_FF_EXPERT = 7168
EPS = 1e-6

MAIN_COLS = 4 * 1024 + 2 * 512 + 2 * 1024
GATE_COLS = 2 * D_MODEL

LANE = 128
VMEM_LIMIT = 56 * 1024 * 1024

TM = 1024
TN_IN = 512
TN_OUT = 512
TF_DENSE = 256
TF_MOE = 256
MOE_SUB = 256
ATT_CHUNK = 128
ATT_TB = 1024
ROW_BATCH = 256


def _cparams(sem):
    return pltpu.CompilerParams(dimension_semantics=sem, vmem_limit_bytes=VMEM_LIMIT)


def _rms(x, w):
    return x * lax.rsqrt(jnp.mean(x * x, axis=-1, keepdims=True) + EPS) * w


def _sigmoid(x):
    e = jnp.exp(-jnp.abs(x))
    r = 1.0 / (1.0 + e)
    return jnp.where(x >= 0, r, e * r)


def _log_sigmoid(x):
    return jnp.minimum(x, 0.0) - jnp.log1p(jnp.exp(-jnp.abs(x)))


def _bdot(a, b):
    return jnp.dot(a, b, preferred_element_type=F32)


def _inproj_main_kernel(x_ref, nw_ref, w_ref, lb_ref, u_ref, logf_ref, h_scr, *, tn):
    j = pl.program_id(1)

    @pl.when(j == 0)
    def _():
        h_scr[...] = _rms(x_ref[...], nw_ref[...]).astype(BF16)

    acc = _bdot(h_scr[...], w_ref[...].astype(BF16))
    per = 1024 // tn
    half = 512 // tn
    scale = HEAD_K ** -0.5
    b_f, b_i, b_g = per, 2 * per, 3 * per
    b_qb = 4 * per
    b_kb = b_qb + half
    b_vb = b_kb + half
    b_gb = b_vb + per

    is_q = (j < b_f) | ((j >= b_qb) & (j < b_kb))
    is_f = (j >= b_f) & (j < b_i)
    is_gate = ((j >= b_g) & (j < b_qb)) | (j >= b_gb)
    is_plain = jnp.logical_not(is_q | is_f | is_gate)

    @pl.when(is_q)
    def _():
        u_ref[...] = (acc * scale).astype(BF16)

    @pl.when(is_plain)
    def _():
        u_ref[...] = acc.astype(BF16)

    @pl.when(is_gate)
    def _():
        u_ref[...] = (acc * _sigmoid(acc)).astype(BF16)

    @pl.when(is_f)
    def _():
        lb = lb_ref[...]
        e = jnp.exp(-jnp.abs(acc))
        r = 1.0 / (1.0 + e)
        sig_neg = jnp.where(acc >= 0, e * r, r)
        ls = jnp.minimum(acc, 0.0) - jnp.log1p(e)
        a = jnp.log(lb)
        b = jnp.log1p(-lb) + ls
        m = jnp.maximum(a, b)
        logf_ref[...] = m + jnp.log1p(jnp.exp(-jnp.abs(a - b)))
        u_ref[...] = ((1.0 - lb) * sig_neg).astype(BF16)


def _inproj_main(x, nw, w_in, layer, lb, *, tm, tn):
    t = x.shape[0]
    per = 1024 // tn
    nj = MAIN_COLS // tn
    kern = functools.partial(_inproj_main_kernel, tn=tn)
    return pl.pallas_call(
        kern,
        grid=(t // tm, nj),
        in_specs=[
            pl.BlockSpec((tm, D_MODEL), lambda i, j: (i, 0)),
            pl.BlockSpec((1, D_MODEL), lambda i, j: (0, 0)),
            pl.BlockSpec((None, D_MODEL, tn), lambda i, j: (layer, 0, j)),
            pl.BlockSpec((1, tn), lambda i, j: (0, jnp.clip(j - per, 0, per - 1))),
        ],
        out_specs=[
            pl.BlockSpec((tm, tn), lambda i, j: (i, j)),
            pl.BlockSpec((tm, tn), lambda i, j: (i, jnp.clip(j - per, 0, per - 1))),
        ],
        out_shape=[
            jax.ShapeDtypeStruct((t, MAIN_COLS), BF16),
            jax.ShapeDtypeStruct((t, HGRN_KEY), F32),
        ],
        scratch_shapes=[pltpu.VMEM((tm, D_MODEL), BF16)],
        compiler_params=_cparams(("parallel", "arbitrary")),
        name="inproj_main",
    )(x, nw, w_in, lb)


def _inproj_tail_kernel(x_ref, nw_ref, wm_ref, wgk_ref, up_ref, upb_ref, gates_ref, loggk_ref, h_scr):
    j = pl.program_id(1)

    @pl.when(j == 0)
    def _():
        h = _rms(x_ref[...], nw_ref[...]).astype(BF16)
        h_scr[...] = h
        code = _bdot(h, wgk_ref[...].astype(BF16))
        lg = _bdot(code.astype(BF16), up_ref[...].astype(BF16)) + upb_ref[...]
        loggk_ref[...] = _log_sigmoid(lg) * (1.0 / GLA_GATE_NORMALIZER)

    acc = _bdot(h_scr[...], wm_ref[...].astype(BF16))
    gates_ref[...] = _sigmoid(acc).astype(BF16)


def _inproj_tail(x, nw, w_m, w_gk, up_w, up_b, *, tm, tn):
    t = x.shape[0]
    return pl.pallas_call(
        _inproj_tail_kernel,
        grid=(t // tm, GATE_COLS // tn),
        in_specs=[
            pl.BlockSpec((tm, D_MODEL), lambda i, j: (i, 0)),
            pl.BlockSpec((1, D_MODEL), lambda i, j: (0, 0)),
            pl.BlockSpec((D_MODEL, tn), lambda i, j: (0, j)),
            pl.BlockSpec((D_MODEL, GLA_GATE_RANK), lambda i, j: (0, 0)),
            pl.BlockSpec((GLA_GATE_RANK, GLA_KEY), lambda i, j: (0, 0)),
            pl.BlockSpec((1, GLA_KEY), lambda i, j: (0, 0)),
        ],
        out_specs=[
            pl.BlockSpec((tm, tn), lambda i, j: (i, j)),
            pl.BlockSpec((tm, GLA_KEY), lambda i, j: (i, 0)),
        ],
        out_shape=[
            jax.ShapeDtypeStruct((t, GATE_COLS), BF16),
            jax.ShapeDtypeStruct((t, GLA_KEY), F32),
        ],
        scratch_shapes=[pltpu.VMEM((tm, D_MODEL), BF16)],
        compiler_params=_cparams(("parallel", "arbitrary")),
        name="inproj_tail",
    )(x, nw, w_m, w_gk, up_w, up_b)


def _recurrence_constants(c):
    levels = int(np.log2(c))
    t = np.arange(c)[:, None]
    s = np.arange(c)[None, :]
    mats = [(s <= t), (s > t)]
    masks = [np.eye(c, dtype=bool)]
    for l in range(levels):
        b = 1 << l
        tb = (t // (2 * b)) * 2 * b + b - 1
        is_i = (t // b) % 2 == 1
        mats.append(np.where(is_i, (s > tb) & (s <= t), (s > t) & (s <= tb)))
        i = np.arange(c)[:, None]
        j = np.arange(c)[None, :]
        masks.append((i // (2 * b) == j // (2 * b)) & ((i // b) % 2 == 1) & ((j // b) % 2 == 0))
    m = np.concatenate(mats, axis=0).astype(np.float32)
    m2 = np.concatenate([m, m], axis=1)
    return jnp.asarray(m2, BF16), jnp.asarray(np.stack(masks).astype(np.float32))


_NT = (((1,), (1,)), ((), ()))
_TN = (((0,), (0,)), ((), ()))


def _recurrence_kernel(q_ref, k_ref, g_ref, v_ref, sg_ref, nw_ref, m2_ref, masks_ref, y_ref, st_ref,
                       *, c, hb, dk, dv, nchunk):
    levels = int(np.log2(c))

    @pl.when(pl.program_id(1) == 0)
    def _():
        st_ref[...] = jnp.zeros_like(st_ref)

    row = lax.broadcasted_iota(jnp.int32, (c, dk), 0)
    nw = nw_ref[...]

    def chunk_body(ci, carry):
        r0 = pl.multiple_of(ci * c, c)
        for hh in range(hb):
            ks = slice(hh * dk, (hh + 1) * dk)
            vs = slice(hh * dv, (hh + 1) * dv)
            q = q_ref[pl.ds(r0, c), ks]
            k = k_ref[pl.ds(r0, c), ks]
            g = g_ref[pl.ds(r0, c), ks]
            v = v_ref[pl.ds(r0, c), vs]
            g_hi = g.astype(BF16)
            g_lo = (g - g_hi.astype(F32)).astype(BF16)
            d_all = _bdot(m2_ref[...], jnp.concatenate([g_hi, g_lo], axis=0))
            e_all = jnp.exp(d_all)
            qf = q.astype(F32)
            kf = k.astype(F32)
            e_cum = e_all[0:c]
            q_in = (qf * e_cum).astype(BF16)
            k_out = (kf * e_all[c:2 * c]).astype(BF16)
            scores = lax.dot_general(q, k, _NT, preferred_element_type=F32) * masks_ref[0]
            for l in range(levels):
                b = 1 << l
                if b >= 8:
                    parts = [(qf if (r // b) % 2 == 1 else kf)[r:r + b] for r in range(0, c, b)]
                    qk = jnp.concatenate(parts, axis=0)
                else:
                    qk = jnp.where(((row >> l) & 1) == 1, qf, kf)
                x = (qk * e_all[(l + 2) * c:(l + 3) * c]).astype(BF16)
                scores = scores + lax.dot_general(x, x, _NT, preferred_element_type=F32) * masks_ref[l + 1]
            st = st_ref[hh]
            o = _bdot(scores.astype(BF16), v) + lax.dot_general(
                q_in, st.astype(BF16), _NT, preferred_element_type=F32)
            st_ref[hh] = st * e_cum[c - 1:c, :] + lax.dot_general(
                v, k_out, _TN, preferred_element_type=F32)
            o = o * lax.rsqrt(jnp.mean(o * o, axis=-1, keepdims=True) + EPS) * nw
            y_ref[pl.ds(r0, c), vs] = (o * sg_ref[pl.ds(r0, c), vs].astype(F32)).astype(BF16)
        return carry

    lax.fori_loop(0, nchunk, chunk_body, 0)


def _recurrence(q_arr, q_off, k_arr, k_off, g_arr, v_arr, v_off, sg_arr, sg_off, nw, consts,
                *, heads, dv, hb, tb, c):
    t = q_arr.shape[0]
    dk = HEAD_K
    m2, masks = consts
    kern = functools.partial(_recurrence_kernel, c=c, hb=hb, dk=dk, dv=dv, nchunk=tb // c)
    wk, wv = hb * dk, hb * dv
    return pl.pallas_call(
        kern,
        grid=(heads // hb, t // tb),
        in_specs=[
            pl.BlockSpec((tb, wk), lambda h, i: (i, q_off + h)),
            pl.BlockSpec((tb, wk), lambda h, i: (i, k_off + h)),
            pl.BlockSpec((tb, wk), lambda h, i: (i, h)),
            pl.BlockSpec((tb, wv), lambda h, i: (i, v_off + h)),
            pl.BlockSpec((tb, wv), lambda h, i: (i, sg_off + h)),
            pl.BlockSpec((1, dv), lambda h, i: (0, 0)),
            pl.BlockSpec(m2.shape, lambda h, i: (0, 0)),
            pl.BlockSpec(masks.shape, lambda h, i: (0, 0, 0)),
        ],
        out_specs=pl.BlockSpec((tb, wv), lambda h, i: (i, h)),
        out_shape=jax.ShapeDtypeStruct((t, heads * dv), BF16),
        scratch_shapes=[pltpu.VMEM((hb, dv, dk), F32)],
        compiler_params=_cparams(("parallel", "arbitrary")),
        name="recurrence_dv%d" % dv,
    )(q_arr, k_arr, g_arr, v_arr, sg_arr, nw, m2, masks)


def _merge_kernel(ya_ref, yb_ref, wa_ref, wb_ref, ga_ref, gb_ref, o_ref):
    pa = _bdot(ya_ref[...], wa_ref[...].astype(BF16))
    pb = _bdot(yb_ref[...], wb_ref[...].astype(BF16))
    o_ref[...] = (ga_ref[...].astype(F32) * pa + gb_ref[...].astype(F32) * pb).astype(BF16)


def _merge(ya, yb, wa, wb, layer, gates, *, tm, tn):
    t = ya.shape[0]
    nj = D_MODEL // tn
    return pl.pallas_call(
        _merge_kernel,
        grid=(t // tm, nj),
        in_specs=[
            pl.BlockSpec((tm, HGRN_VAL), lambda i, j: (i, 0)),
            pl.BlockSpec((tm, GLA_VAL), lambda i, j: (i, 0)),
            pl.BlockSpec((None, HGRN_VAL, tn), lambda i, j: (layer, 0, j)),
            pl.BlockSpec((None, GLA_VAL, tn), lambda i, j: (layer, 0, j)),
            pl.BlockSpec((tm, tn), lambda i, j: (i, j)),
            pl.BlockSpec((tm, tn), lambda i, j: (i, nj + j)),
        ],
        out_specs=pl.BlockSpec((tm, tn), lambda i, j: (i, j)),
        out_shape=jax.ShapeDtypeStruct((t, D_MODEL), BF16),
        compiler_params=_cparams(("parallel", "arbitrary")),
        name="merge",
    )(ya, yb, wa, wb, gates, gates)


def _outproj_kernel(m_ref, w_ref, x_ref, o_ref):
    o_ref[...] = x_ref[...] + _bdot(m_ref[...], w_ref[...].astype(BF16))


def _outproj(merged, w_out, layer, x, *, tm, tn):
    t = x.shape[0]
    return pl.pallas_call(
        _outproj_kernel,
        grid=(t // tm, D_MODEL // tn),
        in_specs=[
            pl.BlockSpec((tm, D_MODEL), lambda i, j: (i, 0)),
            pl.BlockSpec((None, D_MODEL, tn), lambda i, j: (layer, 0, j)),
            pl.BlockSpec((tm, tn), lambda i, j: (i, j)),
        ],
        out_specs=pl.BlockSpec((tm, tn), lambda i, j: (i, j)),
        out_shape=jax.ShapeDtypeStruct((t, D_MODEL), F32),
        compiler_params=_cparams(("parallel", "arbitrary")),
        name="outproj",
    )(merged, w_out, x)


def _dense_ffn_kernel(x_ref, nw_ref, wg_ref, wu_ref, wd_ref, o_ref, h_scr):
    j = pl.program_id(1)

    @pl.when(j == 0)
    def _():
        x = x_ref[...]
        h_scr[...] = _rms(x, nw_ref[...]).astype(BF16)
        o_ref[...] = x

    h = h_scr[...]
    g = _bdot(h, wg_ref[...].astype(BF16))
    u = _bdot(h, wu_ref[...].astype(BF16))
    a = (g * _sigmoid(g) * u).astype(BF16)
    o_ref[...] += _bdot(a, wd_ref[...].astype(BF16))


def _dense_ffn(x, nw, wg, wu, wd, idx, *, tm, tf):
    t = x.shape[0]
    return pl.pallas_call(
        _dense_ffn_kernel,
        grid=(t // tm, D_FF_DENSE // tf),
        in_specs=[
            pl.BlockSpec((tm, D_MODEL), lambda i, j: (i, 0)),
            pl.BlockSpec((1, D_MODEL), lambda i, j: (0, 0)),
            pl.BlockSpec((None, D_MODEL, tf), lambda i, j: (idx, 0, j)),
            pl.BlockSpec((None, D_MODEL, tf), lambda i, j: (idx, 0, j)),
            pl.BlockSpec((None, tf, D_MODEL), lambda i, j: (idx, j, 0)),
        ],
        out_specs=pl.BlockSpec((tm, D_MODEL), lambda i, j: (i, 0)),
        out_shape=jax.ShapeDtypeStruct((t, D_MODEL), F32),
        scratch_shapes=[pltpu.VMEM((tm, D_MODEL), BF16)],
        compiler_params=_cparams(("parallel", "arbitrary")),
        name="dense_ffn",
    )(x, nw, wg, wu, wd)


def _router_kernel(x_ref, nw_ref, rw_ref, rb_ref, h_ref, idx_ref, wts_ref):
    h = _rms(x_ref[...], nw_ref[...])
    h_ref[...] = h
    logits = jnp.dot(h, rw_ref[...], preferred_element_type=F32,
                     precision=lax.Precision.HIGHEST) + rb_ref[...]
    lane_i = lax.broadcasted_iota(jnp.int32, logits.shape, 1)
    lane = lane_i.astype(F32)
    m1 = jnp.max(logits, axis=-1, keepdims=True)
    i1 = jnp.min(jnp.where(logits == m1, lane, float(LANE)), axis=-1, keepdims=True)
    rest = jnp.where(lane == i1, -jnp.inf, logits)
    m2 = jnp.max(rest, axis=-1, keepdims=True)
    i2 = jnp.min(jnp.where(rest == m2, lane, float(LANE)), axis=-1, keepdims=True)
    e = jnp.exp(m2 - m1)
    w1 = 1.0 / (1.0 + e)
    w2 = e * w1
    idx_ref[...] = jnp.where(lane_i == 0, i1, jnp.where(lane_i == 1, i2, 0.0)).astype(jnp.int32)
    wts_ref[...] = jnp.where(lane_i == 0, w1, jnp.where(lane_i == 1, w2, 0.0))


def _router(x, nw, rw_pad, rb_pad, *, tm):
    t = x.shape[0]
    return pl.pallas_call(
        _router_kernel,
        grid=(t // tm,),
        in_specs=[
            pl.BlockSpec((tm, D_MODEL), lambda i: (i, 0)),
            pl.BlockSpec((1, D_MODEL), lambda i: (0, 0)),
            pl.BlockSpec((D_MODEL, LANE), lambda i: (0, 0)),
            pl.BlockSpec((1, LANE), lambda i: (0, 0)),
        ],
        out_specs=[
            pl.BlockSpec((tm, D_MODEL), lambda i: (i, 0)),
            pl.BlockSpec((tm, LANE), lambda i: (i, 0)),
            pl.BlockSpec((tm, LANE), lambda i: (i, 0)),
        ],
        out_shape=[
            jax.ShapeDtypeStruct((t, D_MODEL), F32),
            jax.ShapeDtypeStruct((t, LANE), jnp.int32),
            jax.ShapeDtypeStruct((t, LANE), F32),
        ],
        compiler_params=_cparams(("parallel",)),
        name="router",
    )(x, nw, rw_pad, rb_pad)


def _dispatch_kernel(dest_ref, pad_start_ref, pad_count_ref, h_hbm, xs_hbm, zero_scr, sem, *, nb, n_slots):
    step = pl.program_id(0)
    n_slot_steps = n_slots // nb

    def row_copy(src_ref, src_row, dst_row):
        return pltpu.make_async_copy(src_ref.at[pl.ds(src_row, 1), :],
                                     xs_hbm.at[pl.ds(dst_row, 1), :], sem.at[0])

    @pl.when(step < n_slot_steps)
    def _():
        base = step * nb

        def issue(r, c):
            s = base + r
            row_copy(h_hbm, s // TOP_K, dest_ref[s]).start()
            return c

        lax.fori_loop(0, nb, issue, 0)

        def drain(r, c):
            row_copy(h_hbm, 0, 0).wait()
            return c

        lax.fori_loop(0, nb, drain, 0)

    @pl.when(step >= n_slot_steps)
    def _():
        e = step - n_slot_steps
        zero_scr[...] = jnp.zeros_like(zero_scr)
        start = pad_start_ref[e]
        n = pad_count_ref[e]

        def issue(r, c):
            row_copy(zero_scr, 0, start + r).start()
            return c

        lax.fori_loop(0, n, issue, 0)

        def drain(r, c):
            row_copy(zero_scr, 0, 0).wait()
            return c

        lax.fori_loop(0, n, drain, 0)


def _dispatch(dest, pad_start, pad_count, h, t_pad, *, nb):
    n_slots = dest.shape[0]
    kern = functools.partial(_dispatch_kernel, nb=nb, n_slots=n_slots)
    grid_spec = pltpu.PrefetchScalarGridSpec(
        num_scalar_prefetch=3,
        grid=(n_slots // nb + N_EXPERTS,),
        in_specs=[pl.BlockSpec(memory_space=pl.ANY)],
        out_specs=pl.BlockSpec(memory_space=pl.ANY),
        scratch_shapes=[pltpu.VMEM((8, D_MODEL), F32), pltpu.SemaphoreType.DMA((1,))],
    )
    return pl.pallas_call(
        kern,
        grid_spec=grid_spec,
        out_shape=jax.ShapeDtypeStruct((t_pad, D_MODEL), F32),
        compiler_params=_cparams(("arbitrary",)),
        name="dispatch",
    )(dest, pad_start, pad_count, h)


def _moe_ffn_kernel(te_ref, tv_ref, tb_ref, x_ref, wg_ref, wu_ref, wd_ref, o_ref,
                    xb_scr, wg_scr, wu_scr, wd_scr, *, sub):
    i = pl.program_id(0)
    j = pl.program_id(1)
    valid = tv_ref[i]

    @pl.when(valid > 0)
    def _():
        nsub = (valid + sub - 1) // sub

        @pl.when(j == 0)
        def _():
            o_ref[...] = jnp.zeros_like(o_ref)

            def cast_rows(s, c):
                rows = pl.ds(pl.multiple_of(s * sub, sub), sub)
                xb_scr[rows, :] = x_ref[rows, :].astype(BF16)
                return c

            lax.fori_loop(0, nsub, cast_rows, 0)

        wg_scr[...] = wg_ref[...].astype(BF16)
        wu_scr[...] = wu_ref[...].astype(BF16)
        wd_scr[...] = wd_ref[...].astype(BF16)

        def rows_body(s, c):
            rows = pl.ds(pl.multiple_of(s * sub, sub), sub)
            h = xb_scr[rows, :]
            g = _bdot(h, wg_scr[...])
            u = _bdot(h, wu_scr[...])
            a = (g * _sigmoid(g) * u).astype(BF16)
            o_ref[rows, :] += _bdot(a, wd_scr[...])
            return c

        lax.fori_loop(0, nsub, rows_body, 0)


def _moe_ffn(tile_expert, tile_valid, tile_blk, xs, wg, wu, wd, idx, *, tm, tf, sub):
    t_pad = xs.shape[0]
    n_tiles = t_pad // tm
    nf = D_FF_EXPERT // tf
    kern = functools.partial(_moe_ffn_kernel, sub=sub)

    def jj(i, j, tv):
        return jnp.where(tv[i] > 0, j, nf - 1)

    grid_spec = pltpu.PrefetchScalarGridSpec(
        num_scalar_prefetch=3,
        grid=(n_tiles, nf),
        in_specs=[
            pl.BlockSpec((tm, D_MODEL), lambda i, j, te, tv, tb: (tb[i], 0)),
            pl.BlockSpec((None, None, D_MODEL, tf), lambda i, j, te, tv, tb: (idx, te[i], 0, jj(i, j, tv))),
            pl.BlockSpec((None, None, D_MODEL, tf), lambda i, j, te, tv, tb: (idx, te[i], 0, jj(i, j, tv))),
            pl.BlockSpec((None, None, tf, D_MODEL), lambda i, j, te, tv, tb: (idx, te[i], jj(i, j, tv), 0)),
        ],
        out_specs=pl.BlockSpec((tm, D_MODEL), lambda i, j, te, tv, tb: (tb[i], 0)),
        scratch_shapes=[
            pltpu.VMEM((tm, D_MODEL), BF16),
            pltpu.VMEM((D_MODEL, tf), BF16),
            pltpu.VMEM((D_MODEL, tf), BF16),
            pltpu.VMEM((tf, D_MODEL), BF16),
        ],
    )
    return pl.pallas_call(
        kern,
        grid_spec=grid_spec,
        out_shape=jax.ShapeDtypeStruct((t_pad, D_MODEL), F32),
        compiler_params=_cparams(("arbitrary", "arbitrary")),
        name="moe_ffn",
    )(tile_expert, tile_valid, tile_blk, xs, wg, wu, wd)


def _combine_kernel(dest_ref, x_ref, wts_ref, fw_ref, ys_hbm, o_ref, ya_scr, yb_scr, sem,
                    *, nb, final_norm):
    base = pl.program_id(0) * nb

    def row_copy(src_row, dst_ref, r):
        return pltpu.make_async_copy(ys_hbm.at[pl.ds(src_row, 1), :],
                                     dst_ref.at[pl.ds(r, 1), :], sem.at[0])

    def issue(r, c):
        s = (base + r) * TOP_K
        row_copy(dest_ref[s], ya_scr, r).start()
        row_copy(dest_ref[s + 1], yb_scr, r).start()
        return c

    lax.fori_loop(0, nb, issue, 0)

    def drain(r, c):
        row_copy(0, ya_scr, 0).wait()
        row_copy(0, yb_scr, 0).wait()
        return c

    lax.fori_loop(0, nb, drain, 0)

    w = wts_ref[...]
    out = x_ref[...] + w[:, 0:1] * ya_scr[...] + w[:, 1:2] * yb_scr[...]
    if final_norm:
        out = _rms(out, fw_ref[...])
    o_ref[...] = out


def _combine(dest, x, wts, fw, ys, *, nb, final_norm):
    t = x.shape[0]
    kern = functools.partial(_combine_kernel, nb=nb, final_norm=final_norm)
    grid_spec = pltpu.PrefetchScalarGridSpec(
        num_scalar_prefetch=1,
        grid=(t // nb,),
        in_specs=[
            pl.BlockSpec((nb, D_MODEL), lambda i, d: (i, 0)),
            pl.BlockSpec((nb, LANE), lambda i, d: (i, 0)),
            pl.BlockSpec((1, D_MODEL), lambda i, d: (0, 0)),
            pl.BlockSpec(memory_space=pl.ANY),
        ],
        out_specs=pl.BlockSpec((nb, D_MODEL), lambda i, d: (i, 0)),
        scratch_shapes=[
            pltpu.VMEM((nb, D_MODEL), F32),
            pltpu.VMEM((nb, D_MODEL), F32),
            pltpu.SemaphoreType.DMA((1,)),
        ],
    )
    return pl.pallas_call(
        kern,
        grid_spec=grid_spec,
        out_shape=jax.ShapeDtypeStruct((t, D_MODEL), F32),
        compiler_params=_cparams(("arbitrary",)),
        name="combine",
    )(dest, x, wts, fw, ys)


def _routing_tables(idx2, tm, sub):
    n_slots = idx2.shape[0] * TOP_K
    e_flat = idx2.reshape(-1)
    onehot = (e_flat[:, None] == jnp.arange(N_EXPERTS, dtype=jnp.int32)[None, :]).astype(jnp.int32)
    csum = jnp.cumsum(onehot, axis=0)
    counts = csum[-1]
    rank = jnp.sum(onehot * csum, axis=1) - 1
    padded = ((counts + tm - 1) // tm) * tm
    gend = jnp.cumsum(padded)
    gstart = gend - padded
    dest = (jnp.sum(onehot * gstart[None, :], axis=1) + rank).astype(jnp.int32)

    n_tiles = n_slots // tm + N_EXPERTS
    t_pad = n_tiles * tm
    tile_row0 = jnp.arange(n_tiles, dtype=jnp.int32) * tm
    n_used = gend[-1] // tm
    tile_blk = jnp.minimum(jnp.arange(n_tiles, dtype=jnp.int32), n_used - 1)
    blk_row0 = tile_blk * tm
    tile_expert = jnp.sum((blk_row0[:, None] >= gend[None, :]).astype(jnp.int32), axis=1)
    tile_expert = jnp.minimum(tile_expert, N_EXPERTS - 1)
    valid = jnp.clip(gstart[tile_expert] + counts[tile_expert] - tile_row0, 0, tm)
    tile_valid = jnp.where(jnp.arange(n_tiles) < n_used, valid, 0).astype(jnp.int32)

    pad_start = (gstart + counts).astype(jnp.int32)
    pad_count = (((counts + sub - 1) // sub) * sub - counts).astype(jnp.int32)
    return (dest, pad_start, pad_count, tile_expert.astype(jnp.int32), tile_valid,
            tile_blk.astype(jnp.int32), t_pad)


def _moe_block(x, nw, rw, rb, wg, wu, wd, idx, fw, final_norm):
    t = x.shape[0]
    tm = min(TM, t)
    rw_pad = jnp.pad(rw, ((0, 0), (0, LANE - N_EXPERTS)))
    rb_pad = jnp.pad(rb, (0, LANE - N_EXPERTS), constant_values=-1e30).reshape(1, LANE)
    h, idx_l, wts = _router(x, nw, rw_pad, rb_pad, tm=tm)
    dest, pad_start, pad_count, tile_expert, tile_valid, tile_blk, t_pad = _routing_tables(
        idx_l[:, :TOP_K], tm, MOE_SUB)
    xs = _dispatch(dest, pad_start, pad_count, h, t_pad, nb=ROW_BATCH)
    ys = _moe_ffn(tile_expert, tile_valid, tile_blk, xs, wg, wu, wd, idx, tm=tm, tf=TF_MOE, sub=MOE_SUB)
    return _combine(dest, x, wts, fw, ys, nb=ROW_BATCH, final_norm=final_norm)


def _final_norm_kernel(x_ref, w_ref, o_ref):
    o_ref[...] = _rms(x_ref[...], w_ref[...])


def _final_norm(x, w, *, tm):
    t = x.shape[0]
    return pl.pallas_call(
        _final_norm_kernel,
        grid=(t // tm,),
        in_specs=[pl.BlockSpec((tm, D_MODEL), lambda i: (i, 0)),
                  pl.BlockSpec((1, D_MODEL), lambda i: (0, 0))],
        out_specs=pl.BlockSpec((tm, D_MODEL), lambda i: (i, 0)),
        out_shape=jax.ShapeDtypeStruct((t, D_MODEL), F32),
        compiler_params=_cparams(("parallel",)),
        name="final_norm",
    )(x, w)


def _mixer(x, layer, mix_norm_w, w_in, lower_bound_logits, gk_up_w, gk_up_b, hgrn_norm_w, gla_norm_w,
           w_proj_a, w_proj_b, w_out, consts):
    t = x.shape[0]
    tm = min(TM, t)
    tb = min(ATT_TB, t)
    nw = mix_norm_w[layer].reshape(1, D_MODEL)
    bounds = jnp.cumsum(jax.nn.softmax(lower_bound_logits.astype(F32), axis=0), axis=0)
    lb = (bounds[layer] - bounds[0]).reshape(1, HGRN_KEY)
    u, logf = _inproj_main(x, nw, w_in, layer, lb, tm=tm, tn=TN_IN)
    w_tail = w_in[layer, :, MAIN_COLS:]
    w_gk = w_tail[:, :GLA_GATE_RANK]
    w_m = w_tail[:, GLA_GATE_RANK:]
    gates, loggk = _inproj_tail(x, nw, w_m, w_gk, gk_up_w[layer], gk_up_b[layer].reshape(1, GLA_KEY),
                                tm=tm, tn=TN_IN)
    hb_a, hb_b = 2, 2
    wa = hb_a * HEAD_K
    y_a = _recurrence(u, 0, u, 1024 // wa, logf, u, 2048 // wa, u, 3072 // wa,
                      hgrn_norm_w[layer].reshape(1, -1), consts,
                      heads=HGRN_HEADS, dv=128, hb=hb_a, tb=tb, c=ATT_CHUNK)
    wkb, wvb = hb_b * HEAD_K, hb_b * 256
    y_b = _recurrence(u, 4096 // wkb, u, 4608 // wkb, loggk, u, 5120 // wvb, u, 6144 // wvb,
                      gla_norm_w[layer].reshape(1, -1), consts,
                      heads=GLA_HEADS, dv=256, hb=hb_b, tb=tb, c=ATT_CHUNK)
    merged = _merge(y_a, y_b, w_proj_a, w_proj_b, layer, gates, tm=tm, tn=TN_OUT)
    return _outproj(merged, w_out, layer, x, tm=tm, tn=TN_OUT)


def kernel(x, mix_norm_w, w_in, lower_bound_logits, gk_up_w, gk_up_b, hgrn_norm_w, gla_norm_w, w_proj_a,
           w_proj_b, w_out, ffn_norm_w, dense_w_gate, dense_w_up, dense_w_down, router_w, router_b,
           moe_w_gate, moe_w_up, moe_w_down, final_norm_w):
    b, t, d = x.shape
    xs = x.reshape(b * t, d)
    consts = _recurrence_constants(ATT_CHUNK)
    fw = final_norm_w.reshape(1, D_MODEL)
    fused_final = False
    for layer in range(DEPTH):
        xs = _mixer(xs, layer, mix_norm_w, w_in, lower_bound_logits, gk_up_w, gk_up_b, hgrn_norm_w,
                    gla_norm_w, w_proj_a, w_proj_b, w_out, consts)
        nw = ffn_norm_w[layer].reshape(1, D_MODEL)
        j = layer // 2
        if layer % 2 == 0:
            xs = _dense_ffn(xs, nw, dense_w_gate, dense_w_up, dense_w_down, j,
                            tm=min(TM, xs.shape[0]), tf=TF_DENSE)
        else:
            fused_final = layer == DEPTH - 1
            xs = _moe_block(xs, nw, router_w[j], router_b[j], moe_w_gate, moe_w_up, moe_w_down, j,
                            fw, fused_final)
    if not fused_final:
        xs = _final_norm(xs, fw, tm=min(TM, xs.shape[0]))
    return xs.reshape(b, t, d)
```

```python
import functools

import numpy as np
import jax
import jax.numpy as jnp
from jax import lax
from jax.experimental import pallas as pl
from jax.experimental.pallas import tpu as pltpu

F32 = jnp.float32
BF16 = jnp.bfloat16

D_MODEL = 2048
DEPTH = 2
HGRN_HEADS = 8
HGRN_KEY = 1024
HGRN_VAL = 1024
GLA_HEADS = 4
GLA_KEY = 512
GLA_VAL = 1024
HEAD_K = 128
GLA_GATE_RANK = 16
GLA_GATE_NORMALIZER = 16.0
D_FF_DENSE = 5632
N_EXPERTS = 8
TOP_K = 2
D_FF_EXPERT = 7168
EPS = 1e-6

MAIN_COLS = 4 * 1024 + 2 * 512 + 2 * 1024
GATE_COLS = 2 * D_MODEL

LANE = 128
VMEM_LIMIT = 56 * 1024 * 1024

TM = 1024
TN_IN = 512
TN_OUT = 512
TF_DENSE = 256
TF_MOE = 256
MOE_SUB = 256
ATT_CHUNK = 128
ATT_TB = 1024
ROW_BATCH = 256


def _cparams(sem):
    return pltpu.CompilerParams(dimension_semantics=sem, vmem_limit_bytes=VMEM_LIMIT)


def _rms(x, w):
    return x * lax.rsqrt(jnp.mean(x * x, axis=-1, keepdims=True) + EPS) * w


def _sigmoid(x):
    e = jnp.exp(-jnp.abs(x))
    r = 1.0 / (1.0 + e)
    return jnp.where(x >= 0, r, e * r)


def _log_sigmoid(x):
    return jnp.minimum(x, 0.0) - jnp.log1p(jnp.exp(-jnp.abs(x)))


def _bdot(a, b):
    return jnp.dot(a, b, preferred_element_type=F32)


_NT = (((1,), (1,)), ((), ()))
_TN = (((0,), (0,)), ((), ()))


def _bdot_nt(a, b):
    return lax.dot_general(a, b, _NT, preferred_element_type=F32)


def _inproj_main_kernel(x_ref, nw_ref, w_ref, lb_ref, u_ref, logf_ref, h_scr, *, tn):
    j = pl.program_id(1)

    @pl.when(j == 0)
    def _():
        h_scr[...] = _rms(x_ref[...], nw_ref[...]).astype(BF16)

    acc = _bdot_nt(h_scr[...], w_ref[...].astype(BF16))
    per = 1024 // tn
    half = 512 // tn
    scale = HEAD_K ** -0.5
    b_f, b_i, b_g = per, 2 * per, 3 * per
    b_qb = 4 * per
    b_kb = b_qb + half
    b_vb = b_kb + half
    b_gb = b_vb + per

    is_q = (j < b_f) | ((j >= b_qb) & (j < b_kb))
    is_f = (j >= b_f) & (j < b_i)
    is_gate = ((j >= b_g) & (j < b_qb)) | (j >= b_gb)
    is_plain = jnp.logical_not(is_q | is_f | is_gate)

    @pl.when(is_q)
    def _():
        u_ref[...] = (acc * scale).astype(BF16)

    @pl.when(is_plain)
    def _():
        u_ref[...] = acc.astype(BF16)

    @pl.when(is_gate)
    def _():
        u_ref[...] = (acc * _sigmoid(acc)).astype(BF16)

    @pl.when(is_f)
    def _():
        lb = lb_ref[...]
        e = jnp.exp(-jnp.abs(acc))
        r = 1.0 / (1.0 + e)
        sig_neg = jnp.where(acc >= 0, e * r, r)
        ls = jnp.minimum(acc, 0.0) - jnp.log1p(e)
        a = jnp.log(lb)
        b = jnp.log1p(-lb) + ls
        m = jnp.maximum(a, b)
        logf_ref[...] = m + jnp.log1p(jnp.exp(-jnp.abs(a - b)))
        u_ref[...] = ((1.0 - lb) * sig_neg).astype(BF16)


def _inproj_main(x, nw, w_in, layer, lb, *, tm, tn):
    t = x.shape[0]
    per = 1024 // tn
    nj = MAIN_COLS // tn
    kern = functools.partial(_inproj_main_kernel, tn=tn)
    return pl.pallas_call(
        kern,
        grid=(t // tm, nj),
        in_specs=[
            pl.BlockSpec((tm, D_MODEL), lambda i, j: (i, 0)),
            pl.BlockSpec((1, D_MODEL), lambda i, j: (0, 0)),
            pl.BlockSpec((None, tn, D_MODEL), lambda i, j: (layer, j, 0)),
            pl.BlockSpec((1, tn), lambda i, j: (0, jnp.clip(j - per, 0, per - 1))),
        ],
        out_specs=[
            pl.BlockSpec((tm, tn), lambda i, j: (i, j)),
            pl.BlockSpec((tm, tn), lambda i, j: (i, jnp.clip(j - per, 0, per - 1))),
        ],
        out_shape=[
            jax.ShapeDtypeStruct((t, MAIN_COLS), BF16),
            jax.ShapeDtypeStruct((t, HGRN_KEY), F32),
        ],
        scratch_shapes=[pltpu.VMEM((tm, D_MODEL), BF16)],
        compiler_params=_cparams(("parallel", "arbitrary")),
        name="inproj_main",
    )(x, nw, w_in, lb)


def _inproj_tail_kernel(x_ref, nw_ref, wm_ref, wgk_ref, up_ref, upb_ref, gates_ref, loggk_ref, h_scr):
    j = pl.program_id(1)

    @pl.when(j == 0)
    def _():
        h = _rms(x_ref[...], nw_ref[...]).astype(BF16)
        h_scr[...] = h
        code = _bdot_nt(h, wgk_ref[...].astype(BF16))
        lg = _bdot(code.astype(BF16), up_ref[...].astype(BF16)) + upb_ref[...]
        loggk_ref[...] = _log_sigmoid(lg) * (1.0 / GLA_GATE_NORMALIZER)

    acc = _bdot_nt(h_scr[...], wm_ref[...].astype(BF16))
    gates_ref[...] = _sigmoid(acc).astype(BF16)


def _inproj_tail(x, nw, w_m, w_gk, up_w, up_b, *, tm, tn):
    t = x.shape[0]
    return pl.pallas_call(
        _inproj_tail_kernel,
        grid=(t // tm, GATE_COLS // tn),
        in_specs=[
            pl.BlockSpec((tm, D_MODEL), lambda i, j: (i, 0)),
            pl.BlockSpec((1, D_MODEL), lambda i, j: (0, 0)),
            pl.BlockSpec((tn, D_MODEL), lambda i, j: (j, 0)),
            pl.BlockSpec((GLA_GATE_RANK, D_MODEL), lambda i, j: (0, 0)),
            pl.BlockSpec((GLA_GATE_RANK, GLA_KEY), lambda i, j: (0, 0)),
            pl.BlockSpec((1, GLA_KEY), lambda i, j: (0, 0)),
        ],
        out_specs=[
            pl.BlockSpec((tm, tn), lambda i, j: (i, j)),
            pl.BlockSpec((tm, GLA_KEY), lambda i, j: (i, 0)),
        ],
        out_shape=[
            jax.ShapeDtypeStruct((t, GATE_COLS), BF16),
            jax.ShapeDtypeStruct((t, GLA_KEY), F32),
        ],
        scratch_shapes=[pltpu.VMEM((tm, D_MODEL), BF16)],
        compiler_params=_cparams(("parallel", "arbitrary")),
        name="inproj_tail",
    )(x, nw, w_m, w_gk, up_w, up_b)


def _recurrence_constants(c):
    levels = int(np.log2(c))
    t = np.arange(c)[:, None]
    s = np.arange(c)[None, :]
    mats = [(s <= t), (s > t)]
    masks = [np.eye(c, dtype=bool)]
    for l in range(levels):
        b = 1 << l
        tb = (t // (2 * b)) * 2 * b + b - 1
        is_i = (t // b) % 2 == 1
        mats.append(np.where(is_i, (s > tb) & (s <= t), (s > t) & (s <= tb)))
        i = np.arange(c)[:, None]
        j = np.arange(c)[None, :]
        masks.append((i // (2 * b) == j // (2 * b)) & ((i // b) % 2 == 1) & ((j // b) % 2 == 0))
    m = np.concatenate(mats, axis=0).astype(np.float32)
    m2 = np.concatenate([m, m], axis=1)
    return jnp.asarray(m2, BF16), jnp.asarray(np.stack(masks).astype(np.float32))


def _recurrence_kernel(q_ref, k_ref, g_ref, v_ref, sg_ref, nw_ref, m2_ref, masks_ref, y_ref, st_ref,
                       *, c, hb, dk, dv, nchunk):
    levels = int(np.log2(c))

    @pl.when(pl.program_id(1) == 0)
    def _():
        st_ref[...] = jnp.zeros_like(st_ref)

    row = lax.broadcasted_iota(jnp.int32, (c, dk), 0)
    nw = nw_ref[...]

    def chunk_body(ci, carry):
        r0 = pl.multiple_of(ci * c, c)
        g = g_ref[pl.ds(r0, c), :]
        g_hi = g.astype(BF16)
        g_lo = (g - g_hi.astype(F32)).astype(BF16)
        d_grp = _bdot(m2_ref[...], jnp.concatenate([g_hi, g_lo], axis=0))
        e_grp = jnp.exp(d_grp)
        for hh in range(hb):
            ks = slice(hh * dk, (hh + 1) * dk)
            vs = slice(hh * dv, (hh + 1) * dv)
            q = q_ref[pl.ds(r0, c), ks]
            k = k_ref[pl.ds(r0, c), ks]
            v = v_ref[pl.ds(r0, c), vs]
            e_all = e_grp[:, ks]
            qf = q.astype(F32)
            kf = k.astype(F32)
            e_cum = e_all[0:c]
            q_in = (qf * e_cum).astype(BF16)
            k_out = (kf * e_all[c:2 * c]).astype(BF16)
            scores = lax.dot_general(q, k, _NT, preferred_element_type=F32) * masks_ref[0]
            for l in range(levels):
                b = 1 << l
                if b >= 8:
                    parts = [(qf if (r // b) % 2 == 1 else kf)[r:r + b] for r in range(0, c, b)]
                    qk = jnp.concatenate(parts, axis=0)
                else:
                    qk = jnp.where(((row >> l) & 1) == 1, qf, kf)
                x = (qk * e_all[(l + 2) * c:(l + 3) * c]).astype(BF16)
                scores = scores + lax.dot_general(x, x, _NT, preferred_element_type=F32) * masks_ref[l + 1]
            st = st_ref[hh]
            o = _bdot(scores.astype(BF16), v) + lax.dot_general(
                q_in, st.astype(BF16), _NT, preferred_element_type=F32)
            st_ref[hh] = st * e_cum[c - 1:c, :] + lax.dot_general(
                v, k_out, _TN, preferred_element_type=F32)
            o = o * lax.rsqrt(jnp.mean(o * o, axis=-1, keepdims=True) + EPS) * nw
            y_ref[pl.ds(r0, c), vs] = (o * sg_ref[pl.ds(r0, c), vs].astype(F32)).astype(BF16)
        return carry

    lax.fori_loop(0, nchunk, chunk_body, 0)


def _recurrence(q_arr, q_off, k_arr, k_off, g_arr, v_arr, v_off, sg_arr, sg_off, nw, consts,
                *, heads, dv, hb, tb, c):
    t = q_arr.shape[0]
    dk = HEAD_K
    m2, masks = consts
    kern = functools.partial(_recurrence_kernel, c=c, hb=hb, dk=dk, dv=dv, nchunk=tb // c)
    wk, wv = hb * dk, hb * dv
    return pl.pallas_call(
        kern,
        grid=(heads // hb, t // tb),
        in_specs=[
            pl.BlockSpec((tb, wk), lambda h, i: (i, q_off + h)),
            pl.BlockSpec((tb, wk), lambda h, i: (i, k_off + h)),
            pl.BlockSpec((tb, wk), lambda h, i: (i, h)),
            pl.BlockSpec((tb, wv), lambda h, i: (i, v_off + h)),
            pl.BlockSpec((tb, wv), lambda h, i: (i, sg_off + h)),
            pl.BlockSpec((1, dv), lambda h, i: (0, 0)),
            pl.BlockSpec(m2.shape, lambda h, i: (0, 0)),
            pl.BlockSpec(masks.shape, lambda h, i: (0, 0, 0)),
        ],
        out_specs=pl.BlockSpec((tb, wv), lambda h, i: (i, h)),
        out_shape=jax.ShapeDtypeStruct((t, heads * dv), BF16),
        scratch_shapes=[pltpu.VMEM((hb, dv, dk), F32)],
        compiler_params=_cparams(("parallel", "arbitrary")),
        name="recurrence_dv%d" % dv,
    )(q_arr, k_arr, g_arr, v_arr, sg_arr, nw, m2, masks)


def _merge_kernel(ya_ref, yb_ref, wa_ref, wb_ref, ga_ref, gb_ref, o_ref):
    pa = _bdot(ya_ref[...], wa_ref[...].astype(BF16))
    pb = _bdot(yb_ref[...], wb_ref[...].astype(BF16))
    o_ref[...] = (ga_ref[...].astype(F32) * pa + gb_ref[...].astype(F32) * pb).astype(BF16)


def _merge(ya, yb, wa, wb, layer, gates, *, tm, tn):
    t = ya.shape[0]
    nj = D_MODEL // tn
    return pl.pallas_call(
        _merge_kernel,
        grid=(t // tm, nj),
        in_specs=[
            pl.BlockSpec((tm, HGRN_VAL), lambda i, j: (i, 0)),
            pl.BlockSpec((tm, GLA_VAL), lambda i, j: (i, 0)),
            pl.BlockSpec((None, HGRN_VAL, tn), lambda i, j: (layer, 0, j)),
            pl.BlockSpec((None, GLA_VAL, tn), lambda i, j: (layer, 0, j)),
            pl.BlockSpec((tm, tn), lambda i, j: (i, j)),
            pl.BlockSpec((tm, tn), lambda i, j: (i, nj + j)),
        ],
        out_specs=pl.BlockSpec((tm, tn), lambda i, j: (i, j)),
        out_shape=jax.ShapeDtypeStruct((t, D_MODEL), BF16),
        compiler_params=_cparams(("parallel", "arbitrary")),
        name="merge",
    )(ya, yb, wa, wb, gates, gates)


def _outproj_kernel(m_ref, w_ref, x_ref, o_ref):
    o_ref[...] = x_ref[...] + _bdot(m_ref[...], w_ref[...].astype(BF16))


def _outproj(merged, w_out, layer, x, *, tm, tn):
    t = x.shape[0]
    return pl.pallas_call(
        _outproj_kernel,
        grid=(t // tm, D_MODEL // tn),
        in_specs=[
            pl.BlockSpec((tm, D_MODEL), lambda i, j: (i, 0)),
            pl.BlockSpec((None, D_MODEL, tn), lambda i, j: (layer, 0, j)),
            pl.BlockSpec((tm, tn), lambda i, j: (i, j)),
        ],
        out_specs=pl.BlockSpec((tm, tn), lambda i, j: (i, j)),
        out_shape=jax.ShapeDtypeStruct((t, D_MODEL), F32),
        compiler_params=_cparams(("parallel", "arbitrary")),
        name="outproj",
    )(merged, w_out, x)


def _dense_ffn_kernel(x_ref, nw_ref, wg_ref, wu_ref, wd_ref, o_ref, h_scr):
    j = pl.program_id(1)

    @pl.when(j == 0)
    def _():
        x = x_ref[...]
        h_scr[...] = _rms(x, nw_ref[...]).astype(BF16)
        o_ref[...] = x

    h = h_scr[...]
    g = _bdot(h, wg_ref[...].astype(BF16))
    u = _bdot(h, wu_ref[...].astype(BF16))
    a = (g * _sigmoid(g) * u).astype(BF16)
    o_ref[...] += _bdot(a, wd_ref[...].astype(BF16))


def _dense_ffn(x, nw, wg, wu, wd, idx, *, tm, tf):
    t = x.shape[0]
    return pl.pallas_call(
        _dense_ffn_kernel,
        grid=(t // tm, D_FF_DENSE // tf),
        in_specs=[
            pl.BlockSpec((tm, D_MODEL), lambda i, j: (i, 0)),
            pl.BlockSpec((1, D_MODEL), lambda i, j: (0, 0)),
            pl.BlockSpec((None, D_MODEL, tf), lambda i, j: (idx, 0, j)),
            pl.BlockSpec((None, D_MODEL, tf), lambda i, j: (idx, 0, j)),
            pl.BlockSpec((None, tf, D_MODEL), lambda i, j: (idx, j, 0)),
        ],
        out_specs=pl.BlockSpec((tm, D_MODEL), lambda i, j: (i, 0)),
        out_shape=jax.ShapeDtypeStruct((t, D_MODEL), F32),
        scratch_shapes=[pltpu.VMEM((tm, D_MODEL), BF16)],
        compiler_params=_cparams(("parallel", "arbitrary")),
        name="dense_ffn",
    )(x, nw, wg, wu, wd)


def _router_kernel(x_ref, nw_ref, rw_ref, rb_ref, h_ref, idx_ref, wts_ref):
    h = _rms(x_ref[...], nw_ref[...])
    h_ref[...] = h
    logits = jnp.dot(h, rw_ref[...], preferred_element_type=F32,
                     precision=lax.Precision.HIGHEST) + rb_ref[...]
    lane_i = lax.broadcasted_iota(jnp.int32, logits.shape, 1)
    lane = lane_i.astype(F32)
    m1 = jnp.max(logits, axis=-1, keepdims=True)
    i1 = jnp.min(jnp.where(logits == m1, lane, float(LANE)), axis=-1, keepdims=True)
    rest = jnp.where(lane == i1, -jnp.inf, logits)
    m2 = jnp.max(rest, axis=-1, keepdims=True)
    i2 = jnp.min(jnp.where(rest == m2, lane, float(LANE)), axis=-1, keepdims=True)
    e = jnp.exp(m2 - m1)
    w1 = 1.0 / (1.0 + e)
    w2 = e * w1
    idx_ref[...] = jnp.where(lane_i == 0, i1, jnp.where(lane_i == 1, i2, 0.0)).astype(jnp.int32)
    wts_ref[...] = jnp.where(lane_i == 0, w1, jnp.where(lane_i == 1, w2, 0.0))


def _router(x, nw, rw_pad, rb_pad, *, tm):
    t = x.shape[0]
    return pl.pallas_call(
        _router_kernel,
        grid=(t // tm,),
        in_specs=[
            pl.BlockSpec((tm, D_MODEL), lambda i: (i, 0)),
            pl.BlockSpec((1, D_MODEL), lambda i: (0, 0)),
            pl.BlockSpec((D_MODEL, LANE), lambda i: (0, 0)),
            pl.BlockSpec((1, LANE), lambda i: (0, 0)),
        ],
        out_specs=[
            pl.BlockSpec((tm, D_MODEL), lambda i: (i, 0)),
            pl.BlockSpec((tm, LANE), lambda i: (i, 0)),
            pl.BlockSpec((tm, LANE), lambda i: (i, 0)),
        ],
        out_shape=[
            jax.ShapeDtypeStruct((t, D_MODEL), F32),
            jax.ShapeDtypeStruct((t, LANE), jnp.int32),
            jax.ShapeDtypeStruct((t, LANE), F32),
        ],
        compiler_params=_cparams(("parallel",)),
        name="router",
    )(x, nw, rw_pad, rb_pad)


def _dispatch_kernel(dest_ref, pad_start_ref, pad_count_ref, h_ref, xs_hbm, zero_scr, sem, *, nb, n_tok):
    step = pl.program_id(0)
    n_tok_steps = n_tok // nb

    def row_copy(src_ref, src_row, dst_row):
        return pltpu.make_async_copy(src_ref.at[pl.ds(src_row, 1), :],
                                     xs_hbm.at[pl.ds(dst_row, 1), :], sem.at[0])

    @pl.when(step < n_tok_steps)
    def _():
        base = step * (nb * TOP_K)

        def issue(r, c):
            for kk in range(TOP_K):
                row_copy(h_ref, r, dest_ref[base + r * TOP_K + kk]).start()
            return c

        lax.fori_loop(0, nb, issue, 0)

        def drain(r, c):
            for kk in range(TOP_K):
                row_copy(h_ref, 0, 0).wait()
            return c

        lax.fori_loop(0, nb, drain, 0)

    @pl.when(step >= n_tok_steps)
    def _():
        e = step - n_tok_steps
        zero_scr[...] = jnp.zeros_like(zero_scr)
        start = pad_start_ref[e]
        n = pad_count_ref[e]

        def issue(r, c):
            row_copy(zero_scr, 0, start + r).start()
            return c

        lax.fori_loop(0, n, issue, 0)

        def drain(r, c):
            row_copy(zero_scr, 0, 0).wait()
            return c

        lax.fori_loop(0, n, drain, 0)


def _dispatch(dest, pad_start, pad_count, h, t_pad, *, nb):
    n_tok = h.shape[0]
    n_tok_steps = n_tok // nb
    kern = functools.partial(_dispatch_kernel, nb=nb, n_tok=n_tok)
    grid_spec = pltpu.PrefetchScalarGridSpec(
        num_scalar_prefetch=3,
        grid=(n_tok_steps + N_EXPERTS,),
        in_specs=[pl.BlockSpec((nb, D_MODEL),
                               lambda i, d, ps, pc: (jnp.minimum(i, n_tok_steps - 1), 0))],
        out_specs=pl.BlockSpec(memory_space=pl.ANY),
        scratch_shapes=[pltpu.VMEM((8, D_MODEL), F32), pltpu.SemaphoreType.DMA((1,))],
    )
    return pl.pallas_call(
        kern,
        grid_spec=grid_spec,
        out_shape=jax.ShapeDtypeStruct((t_pad, D_MODEL), F32),
        compiler_params=_cparams(("arbitrary",)),
        name="dispatch",
    )(dest, pad_start, pad_count, h)


def _moe_ffn_kernel(te_ref, tv_ref, tb_ref, x_ref, wg_ref, wu_ref, wd_ref, o_ref,
                    xb_scr, wg_scr, wu_scr, wd_scr, *, sub):
    i = pl.program_id(0)
    j = pl.program_id(1)
    valid = tv_ref[i]

    @pl.when(valid > 0)
    def _():
        nsub = (valid + sub - 1) // sub

        @pl.when(j == 0)
        def _():
            o_ref[...] = jnp.zeros_like(o_ref)

            def cast_rows(s, c):
                rows = pl.ds(pl.multiple_of(s * sub, sub), sub)
                xb_scr[rows, :] = x_ref[rows, :].astype(BF16)
                return c

            lax.fori_loop(0, nsub, cast_rows, 0)

        wg_scr[...] = wg_ref[...].astype(BF16)
        wu_scr[...] = wu_ref[...].astype(BF16)
        wd_scr[...] = wd_ref[...].astype(BF16)

        def rows_chain(row0):
            rows = pl.ds(row0, sub)
            h = xb_scr[rows, :]
            g = _bdot(h, wg_scr[...])
            u = _bdot(h, wu_scr[...])
            a = (g * _sigmoid(g) * u).astype(BF16)
            o_ref[rows, :] += _bdot(a, wd_scr[...])

        def pair_body(p, c):
            base = pl.multiple_of(p * (2 * sub), 2 * sub)
            rows_chain(base)
            rows_chain(base + sub)
            return c

        lax.fori_loop(0, nsub // 2, pair_body, 0)

        @pl.when(nsub % 2 == 1)
        def _():
            rows_chain(pl.multiple_of((nsub - 1) * sub, sub))


def _moe_ffn(tile_expert, tile_valid, tile_blk, xs, wg, wu, wd, idx, *, tm, tf, sub):
    t_pad = xs.shape[0]
    n_tiles = t_pad // tm
    nf = D_FF_EXPERT // tf
    kern = functools.partial(_moe_ffn_kernel, sub=sub)

    def jj(i, j, tv):
        return jnp.where(tv[i] > 0, j, nf - 1)

    grid_spec = pltpu.PrefetchScalarGridSpec(
        num_scalar_prefetch=3,
        grid=(n_tiles, nf),
        in_specs=[
            pl.BlockSpec((tm, D_MODEL), lambda i, j, te, tv, tb: (tb[i], 0)),
            pl.BlockSpec((None, None, D_MODEL, tf), lambda i, j, te, tv, tb: (idx, te[i], 0, jj(i, j, tv))),
            pl.BlockSpec((None, None, D_MODEL, tf), lambda i, j, te, tv, tb: (idx, te[i], 0, jj(i, j, tv))),
            pl.BlockSpec((None, None, tf, D_MODEL), lambda i, j, te, tv, tb: (idx, te[i], jj(i, j, tv), 0)),
        ],
        out_specs=pl.BlockSpec((tm, D_MODEL), lambda i, j, te, tv, tb: (tb[i], 0)),
        scratch_shapes=[
            pltpu.VMEM((tm, D_MODEL), BF16),
            pltpu.VMEM((D_MODEL, tf), BF16),
            pltpu.VMEM((D_MODEL, tf), BF16),
            pltpu.VMEM((tf, D_MODEL), BF16),
        ],
    )
    return pl.pallas_call(
        kern,
        grid_spec=grid_spec,
        out_shape=jax.ShapeDtypeStruct((t_pad, D_MODEL), F32),
        compiler_params=_cparams(("arbitrary", "arbitrary")),
        name="moe_ffn",
    )(tile_expert, tile_valid, tile_blk, xs, wg, wu, wd)


def _combine_kernel(dest_ref, x_ref, wts_ref, fw_ref, ys_hbm, o_ref, ya_scr, yb_scr, sem,
                    *, nb, final_norm):
    base = pl.program_id(0) * nb

    def row_copy(src_row, dst_ref, r):
        return pltpu.make_async_copy(ys_hbm.at[pl.ds(src_row, 1), :],
                                     dst_ref.at[pl.ds(r, 1), :], sem.at[0])

    def issue(r, c):
        s = (base + r) * TOP_K
        row_copy(dest_ref[s], ya_scr, r).start()
        row_copy(dest_ref[s + 1], yb_scr, r).start()
        return c

    lax.fori_loop(0, nb, issue, 0)

    def drain(r, c):
        row_copy(0, ya_scr, 0).wait()
        row_copy(0, yb_scr, 0).wait()
        return c

    lax.fori_loop(0, nb, drain, 0)

    w = wts_ref[...]
    out = x_ref[...] + w[:, 0:1] * ya_scr[...] + w[:, 1:2] * yb_scr[...]
    if final_norm:
        out = _rms(out, fw_ref[...])
    o_ref[...] = out


def _combine(dest, x, wts, fw, ys, *, nb, final_norm):
    t = x.shape[0]
    kern = functools.partial(_combine_kernel, nb=nb, final_norm=final_norm)
    grid_spec = pltpu.PrefetchScalarGridSpec(
        num_scalar_prefetch=1,
        grid=(t // nb,),
        in_specs=[
            pl.BlockSpec((nb, D_MODEL), lambda i, d: (i, 0)),
            pl.BlockSpec((nb, LANE), lambda i, d: (i, 0)),
            pl.BlockSpec((1, D_MODEL), lambda i, d: (0, 0)),
            pl.BlockSpec(memory_space=pl.ANY),
        ],
        out_specs=pl.BlockSpec((nb, D_MODEL), lambda i, d: (i, 0)),
        scratch_shapes=[
            pltpu.VMEM((nb, D_MODEL), F32),
            pltpu.VMEM((nb, D_MODEL), F32),
            pltpu.SemaphoreType.DMA((1,)),
        ],
    )
    return pl.pallas_call(
        kern,
        grid_spec=grid_spec,
        out_shape=jax.ShapeDtypeStruct((t, D_MODEL), F32),
        compiler_params=_cparams(("arbitrary",)),
        name="combine",
    )(dest, x, wts, fw, ys)


def _routing_tables(idx2, tm, sub):
    n_slots = idx2.shape[0] * TOP_K
    e_flat = idx2.reshape(-1)
    onehot = (e_flat[:, None] == jnp.arange(N_EXPERTS, dtype=jnp.int32)[None, :]).astype(jnp.int32)
    csum = jnp.cumsum(onehot, axis=0)
    counts = csum[-1]
    rank = jnp.sum(onehot * csum, axis=1) - 1
    padded = ((counts + tm - 1) // tm) * tm
    gend = jnp.cumsum(padded)
    gstart = gend - padded
    dest = (jnp.sum(onehot * gstart[None, :], axis=1) + rank).astype(jnp.int32)

    n_tiles = n_slots // tm + N_EXPERTS
    t_pad = n_tiles * tm
    tile_row0 = jnp.arange(n_tiles, dtype=jnp.int32) * tm
    n_used = gend[-1] // tm
    tile_blk = jnp.minimum(jnp.arange(n_tiles, dtype=jnp.int32), n_used - 1)
    blk_row0 = tile_blk * tm
    tile_expert = jnp.sum((blk_row0[:, None] >= gend[None, :]).astype(jnp.int32), axis=1)
    tile_expert = jnp.minimum(tile_expert, N_EXPERTS - 1)
    valid = jnp.clip(gstart[tile_expert] + counts[tile_expert] - tile_row0, 0, tm)
    tile_valid = jnp.where(jnp.arange(n_tiles) < n_used, valid, 0).astype(jnp.int32)

    pad_start = (gstart + counts).astype(jnp.int32)
    pad_count = (((counts + sub - 1) // sub) * sub - counts).astype(jnp.int32)
    return (dest, pad_start, pad_count, tile_expert.astype(jnp.int32), tile_valid,
            tile_blk.astype(jnp.int32), t_pad)


def _moe_block(x, nw, rw, rb, wg, wu, wd, idx, fw, final_norm):
    t = x.shape[0]
    tm = min(TM, t)
    rw_pad = jnp.pad(rw, ((0, 0), (0, LANE - N_EXPERTS)))
    rb_pad = jnp.pad(rb, (0, LANE - N_EXPERTS), constant_values=-1e30).reshape(1, LANE)
    h, idx_l, wts = _router(x, nw, rw_pad, rb_pad, tm=tm)
    dest, pad_start, pad_count, tile_expert, tile_valid, tile_blk, t_pad = _routing_tables(
        idx_l[:, :TOP_K], tm, MOE_SUB)
    xs = _dispatch(dest, pad_start, pad_count, h, t_pad, nb=ROW_BATCH)
    ys = _moe_ffn(tile_expert, tile_valid, tile_blk, xs, wg, wu, wd, idx, tm=tm, tf=TF_MOE, sub=MOE_SUB)
    return _combine(dest, x, wts, fw, ys, nb=ROW_BATCH, final_norm=final_norm)


def _final_norm_kernel(x_ref, w_ref, o_ref):
    o_ref[...] = _rms(x_ref[...], w_ref[...])


def _final_norm(x, w, *, tm):
    t = x.shape[0]
    return pl.pallas_call(
        _final_norm_kernel,
        grid=(t // tm,),
        in_specs=[pl.BlockSpec((tm, D_MODEL), lambda i: (i, 0)),
                  pl.BlockSpec((1, D_MODEL), lambda i: (0, 0))],
        out_specs=pl.BlockSpec((tm, D_MODEL), lambda i: (i, 0)),
        out_shape=jax.ShapeDtypeStruct((t, D_MODEL), F32),
        compiler_params=_cparams(("parallel",)),
        name="final_norm",
    )(x, w)


def _mixer(x, layer, mix_norm_w, w_in, lower_bound_logits, gk_up_w, gk_up_b, hgrn_norm_w, gla_norm_w,
           w_proj_a, w_proj_b, w_out, consts):
    t = x.shape[0]
    tm = min(TM, t)
    tb = min(ATT_TB, t)
    nw = mix_norm_w[layer].reshape(1, D_MODEL)
    bounds = jnp.cumsum(jax.nn.softmax(lower_bound_logits.astype(F32), axis=0), axis=0)
    lb = (bounds[layer] - bounds[0]).reshape(1, HGRN_KEY)
    u, logf = _inproj_main(x, nw, w_in, layer, lb, tm=tm, tn=TN_IN)
    w_tail = w_in[layer, MAIN_COLS:, :]
    w_gk = w_tail[:GLA_GATE_RANK]
    w_m = w_tail[GLA_GATE_RANK:]
    gates, loggk = _inproj_tail(x, nw, w_m, w_gk, gk_up_w[layer], gk_up_b[layer].reshape(1, GLA_KEY),
                                tm=tm, tn=TN_IN)
    hb_a, hb_b = 2, 2
    wa = hb_a * HEAD_K
    y_a = _recurrence(u, 0, u, 1024 // wa, logf, u, 2048 // wa, u, 3072 // wa,
                      hgrn_norm_w[layer].reshape(1, -1), consts,
                      heads=HGRN_HEADS, dv=128, hb=hb_a, tb=tb, c=ATT_CHUNK)
    wkb, wvb = hb_b * HEAD_K, hb_b * 256
    y_b = _recurrence(u, 4096 // wkb, u, 4608 // wkb, loggk, u, 5120 // wvb, u, 6144 // wvb,
                      gla_norm_w[layer].reshape(1, -1), consts,
                      heads=GLA_HEADS, dv=256, hb=hb_b, tb=tb, c=ATT_CHUNK)
    merged = _merge(y_a, y_b, w_proj_a, w_proj_b, layer, gates, tm=tm, tn=TN_OUT)
    return _outproj(merged, w_out, layer, x, tm=tm, tn=TN_OUT)


def kernel(x, mix_norm_w, w_in, lower_bound_logits, gk_up_w, gk_up_b, hgrn_norm_w, gla_norm_w, w_proj_a,
           w_proj_b, w_out, ffn_norm_w, dense_w_gate, dense_w_up, dense_w_down, router_w, router_b,
           moe_w_gate, moe_w_up, moe_w_down, final_norm_w):
    b, t, d = x.shape
    xs = x.reshape(b * t, d)
    consts = _recurrence_constants(ATT_CHUNK)
    fw = final_norm_w.reshape(1, D_MODEL)
    fused_final = False
    w_in_t = jnp.swapaxes(w_in, 1, 2)
    for layer in range(DEPTH):
        xs = _mixer(xs, layer, mix_norm_w, w_in_t, lower_bound_logits, gk_up_w, gk_up_b, hgrn_norm_w,
                    gla_norm_w, w_proj_a, w_proj_b, w_out, consts)
        nw = ffn_norm_w[layer].reshape(1, D_MODEL)
        j = layer // 2
        if layer % 2 == 0:
            xs = _dense_ffn(xs, nw, dense_w_gate, dense_w_up, dense_w_down, j,
                            tm=min(TM, xs.shape[0]), tf=TF_DENSE)
        else:
            fused_final = layer == DEPTH - 1
            xs = _moe_block(xs, nw, router_w[j], router_b[j], moe_w_gate, moe_w_up, moe_w_down, j,
                            fw, fused_final)
    if not fused_final:
        xs = _final_norm(xs, fw, tm=min(TM, xs.shape[0]))
    return xs.reshape(b, t, d)
```

```python
import functools

import numpy as np
import jax
import jax.numpy as jnp
from jax import lax
from jax.experimental import pallas as pl
from jax.experimental.pallas import tpu as pltpu

F32 = jnp.float32
BF16 = jnp.bfloat16

D_MODEL = 2048
DEPTH = 2
HGRN_HEADS = 8
HGRN_KEY = 1024
HGRN_VAL = 1024
GLA_HEADS = 4
GLA_KEY = 512
GLA_VAL = 1024
HEAD_K = 128
GLA_GATE_RANK = 16
GLA_GATE_NORMALIZER = 16.0
D_FF_DENSE = 5632
N_EXPERTS = 8
TOP_K = 2
D_FF_EXPERT = 7168
EPS = 1e-6

MAIN_COLS = 4 * 1024 + 2 * 512 + 2 * 1024
GATE_COLS = 2 * D_MODEL

LANE = 128
VMEM_LIMIT = 56 * 1024 * 1024

TM = 1024
TN_IN = 512
TN_OUT = 512
TF_DENSE = 256
TF_MOE = 256
MOE_TM = 2048
MOE_SUB = 256
ATT_CHUNK = 128
ATT_TB = 1024
ROW_BATCH = 256
ROW_CHUNK = 256


def _cparams(sem):
    return pltpu.CompilerParams(dimension_semantics=sem, vmem_limit_bytes=VMEM_LIMIT)


def _rms(x, w):
    return x * lax.rsqrt(jnp.mean(x * x, axis=-1, keepdims=True) + EPS) * w


def _sigmoid(x):
    e = jnp.exp(-jnp.abs(x))
    r = 1.0 / (1.0 + e)
    return jnp.where(x >= 0, r, e * r)


def _log_sigmoid(x):
    return jnp.minimum(x, 0.0) - jnp.log1p(jnp.exp(-jnp.abs(x)))


def _bdot(a, b):
    return jnp.dot(a, b, preferred_element_type=F32)


_NT = (((1,), (1,)), ((), ()))
_TN = (((0,), (0,)), ((), ()))


def _bdot_nt(a, b):
    return lax.dot_general(a, b, _NT, preferred_element_type=F32)


def _row_chunks(tm):
    return [slice(r, r + ROW_CHUNK) for r in range(0, tm, ROW_CHUNK)]


def _inproj_main_kernel(x_ref, nw_ref, w_ref, lb_ref, u_ref, logf_ref, h_scr, *, tn):
    j = pl.program_id(1)
    chunks = _row_chunks(x_ref.shape[0])

    @pl.when(j == 0)
    def _():
        for rows in chunks:
            h_scr[rows, :] = _rms(x_ref[rows, :], nw_ref[...]).astype(BF16)

    def project(rows):
        return _bdot_nt(h_scr[rows, :], w_ref[...].astype(BF16))

    per = 1024 // tn
    half = 512 // tn
    scale = HEAD_K ** -0.5
    b_f, b_i, b_g = per, 2 * per, 3 * per
    b_qb = 4 * per
    b_kb = b_qb + half
    b_vb = b_kb + half
    b_gb = b_vb + per

    is_q = (j < b_f) | ((j >= b_qb) & (j < b_kb))
    is_f = (j >= b_f) & (j < b_i)
    is_gate = ((j >= b_g) & (j < b_qb)) | (j >= b_gb)
    is_plain = jnp.logical_not(is_q | is_f | is_gate)

    @pl.when(is_q)
    def _():
        for rows in chunks:
            u_ref[rows, :] = (project(rows) * scale).astype(BF16)

    @pl.when(is_plain)
    def _():
        for rows in chunks:
            u_ref[rows, :] = project(rows).astype(BF16)

    @pl.when(is_gate)
    def _():
        for rows in chunks:
            acc = project(rows)
            u_ref[rows, :] = (acc * _sigmoid(acc)).astype(BF16)

    @pl.when(is_f)
    def _():
        lb = lb_ref[...]
        log_lb = jnp.log(lb)
        log_1m_lb = jnp.log1p(-lb)
        for rows in chunks:
            acc = project(rows)
            e = jnp.exp(-jnp.abs(acc))
            r = 1.0 / (1.0 + e)
            sig_neg = jnp.where(acc >= 0, e * r, r)
            ls = jnp.minimum(acc, 0.0) - jnp.log1p(e)
            b = log_1m_lb + ls
            m = jnp.maximum(log_lb, b)
            logf_ref[rows, :] = m + jnp.log1p(jnp.exp(-jnp.abs(log_lb - b)))
            u_ref[rows, :] = ((1.0 - lb) * sig_neg).astype(BF16)


def _inproj_main(x, nw, w_in, layer, lb, *, tm, tn):
    t = x.shape[0]
    per = 1024 // tn
    nj = MAIN_COLS // tn
    kern = functools.partial(_inproj_main_kernel, tn=tn)
    return pl.pallas_call(
        kern,
        grid=(t // tm, nj),
        in_specs=[
            pl.BlockSpec((tm, D_MODEL), lambda i, j: (i, 0)),
            pl.BlockSpec((1, D_MODEL), lambda i, j: (0, 0)),
            pl.BlockSpec((None, tn, D_MODEL), lambda i, j: (layer, j, 0)),
            pl.BlockSpec((1, tn), lambda i, j: (0, jnp.clip(j - per, 0, per - 1))),
        ],
        out_specs=[
            pl.BlockSpec((tm, tn), lambda i, j: (i, j)),
            pl.BlockSpec((tm, tn), lambda i, j: (i, jnp.clip(j - per, 0, per - 1))),
        ],
        out_shape=[
            jax.ShapeDtypeStruct((t, MAIN_COLS), BF16),
            jax.ShapeDtypeStruct((t, HGRN_KEY), F32),
        ],
        scratch_shapes=[pltpu.VMEM((tm, D_MODEL), BF16)],
        compiler_params=_cparams(("parallel", "arbitrary")),
        name="inproj_main",
    )(x, nw, w_in, lb)


def _inproj_tail_kernel(x_ref, nw_ref, wm_ref, wgk_ref, up_ref, upb_ref, gates_ref, loggk_ref, h_scr):
    j = pl.program_id(1)
    chunks = _row_chunks(x_ref.shape[0])

    @pl.when(j == 0)
    def _():
        wgk = wgk_ref[...].astype(BF16)
        up = up_ref[...].astype(BF16)
        for rows in chunks:
            h = _rms(x_ref[rows, :], nw_ref[...]).astype(BF16)
            h_scr[rows, :] = h
            code = _bdot_nt(h, wgk)
            lg = _bdot(code.astype(BF16), up) + upb_ref[...]
            loggk_ref[rows, :] = _log_sigmoid(lg) * (1.0 / GLA_GATE_NORMALIZER)

    for rows in chunks:
        acc = _bdot_nt(h_scr[rows, :], wm_ref[...].astype(BF16))
        gates_ref[rows, :] = _sigmoid(acc).astype(BF16)


def _inproj_tail(x, nw, w_m, w_gk, up_w, up_b, *, tm, tn):
    t = x.shape[0]
    return pl.pallas_call(
        _inproj_tail_kernel,
        grid=(t // tm, GATE_COLS // tn),
        in_specs=[
            pl.BlockSpec((tm, D_MODEL), lambda i, j: (i, 0)),
            pl.BlockSpec((1, D_MODEL), lambda i, j: (0, 0)),
            pl.BlockSpec((tn, D_MODEL), lambda i, j: (j, 0)),
            pl.BlockSpec((GLA_GATE_RANK, D_MODEL), lambda i, j: (0, 0)),
            pl.BlockSpec((GLA_GATE_RANK, GLA_KEY), lambda i, j: (0, 0)),
            pl.BlockSpec((1, GLA_KEY), lambda i, j: (0, 0)),
        ],
        out_specs=[
            pl.BlockSpec((tm, tn), lambda i, j: (i, j)),
            pl.BlockSpec((tm, GLA_KEY), lambda i, j: (i, 0)),
        ],
        out_shape=[
            jax.ShapeDtypeStruct((t, GATE_COLS), BF16),
            jax.ShapeDtypeStruct((t, GLA_KEY), F32),
        ],
        scratch_shapes=[pltpu.VMEM((tm, D_MODEL), BF16)],
        compiler_params=_cparams(("parallel", "arbitrary")),
        name="inproj_tail",
    )(x, nw, w_m, w_gk, up_w, up_b)


def _recurrence_constants(c):
    levels = int(np.log2(c))
    t = np.arange(c)[:, None]
    s = np.arange(c)[None, :]
    mats = [(s <= t), (s > t)]
    masks = [np.eye(c, dtype=bool)]
    for l in range(levels):
        b = 1 << l
        tb = (t // (2 * b)) * 2 * b + b - 1
        is_i = (t // b) % 2 == 1
        mats.append(np.where(is_i, (s > tb) & (s <= t), (s > t) & (s <= tb)))
        i = np.arange(c)[:, None]
        j = np.arange(c)[None, :]
        masks.append((i // (2 * b) == j // (2 * b)) & ((i // b) % 2 == 1) & ((j // b) % 2 == 0))
    m = np.concatenate(mats, axis=0).astype(np.float32)
    m2 = np.concatenate([m, m], axis=1)
    return jnp.asarray(m2, BF16), jnp.asarray(np.stack(masks).astype(np.float32))


def _recurrence_kernel(q_ref, k_ref, g_ref, v_ref, sg_ref, nw_ref, m2_ref, masks_ref, y_ref, st_ref,
                       *, c, hb, dk, dv, nchunk):
    levels = int(np.log2(c))

    @pl.when(pl.program_id(1) == 0)
    def _():
        st_ref[...] = jnp.zeros_like(st_ref)

    row = lax.broadcasted_iota(jnp.int32, (c, dk), 0)
    nw = nw_ref[...]

    def chunk_body(ci, carry):
        r0 = pl.multiple_of(ci * c, c)
        g = g_ref[pl.ds(r0, c), :]
        g_hi = g.astype(BF16)
        g_lo = (g - g_hi.astype(F32)).astype(BF16)
        d_grp = _bdot(m2_ref[...], jnp.concatenate([g_hi, g_lo], axis=0))
        e_grp = jnp.exp(d_grp)
        for hh in range(hb):
            ks = slice(hh * dk, (hh + 1) * dk)
            vs = slice(hh * dv, (hh + 1) * dv)
            q = q_ref[pl.ds(r0, c), ks]
            k = k_ref[pl.ds(r0, c), ks]
            v = v_ref[pl.ds(r0, c), vs]
            e_all = e_grp[:, ks]
            qf = q.astype(F32)
            kf = k.astype(F32)
            e_cum = e_all[0:c]
            q_in = (qf * e_cum).astype(BF16)
            k_out = (kf * e_all[c:2 * c]).astype(BF16)
            scores = lax.dot_general(q, k, _NT, preferred_element_type=F32) * masks_ref[0]
            for l in range(levels):
                b = 1 << l
                if b >= 8:
                    parts = [(qf if (r // b) % 2 == 1 else kf)[r:r + b] for r in range(0, c, b)]
                    qk = jnp.concatenate(parts, axis=0)
                else:
                    qk = jnp.where(((row >> l) & 1) == 1, qf, kf)
                x = (qk * e_all[(l + 2) * c:(l + 3) * c]).astype(BF16)
                scores = scores + lax.dot_general(x, x, _NT, preferred_element_type=F32) * masks_ref[l + 1]
            st = st_ref[hh]
            o = _bdot(scores.astype(BF16), v) + lax.dot_general(
                q_in, st.astype(BF16), _NT, preferred_element_type=F32)
            st_ref[hh] = st * e_cum[c - 1:c, :] + lax.dot_general(
                v, k_out, _TN, preferred_element_type=F32)
            o = o * lax.rsqrt(jnp.mean(o * o, axis=-1, keepdims=True) + EPS) * nw
            y_ref[pl.ds(r0, c), vs] = (o * sg_ref[pl.ds(r0, c), vs].astype(F32)).astype(BF16)
        return carry

    lax.fori_loop(0, nchunk, chunk_body, 0)


def _recurrence(q_arr, q_off, k_arr, k_off, g_arr, v_arr, v_off, sg_arr, sg_off, nw, consts,
                *, heads, dv, hb, tb, c):
    t = q_arr.shape[0]
    dk = HEAD_K
    m2, masks = consts
    kern = functools.partial(_recurrence_kernel, c=c, hb=hb, dk=dk, dv=dv, nchunk=tb // c)
    wk, wv = hb * dk, hb * dv
    return pl.pallas_call(
        kern,
        grid=(heads // hb, t // tb),
        in_specs=[
            pl.BlockSpec((tb, wk), lambda h, i: (i, q_off + h)),
            pl.BlockSpec((tb, wk), lambda h, i: (i, k_off + h)),
            pl.BlockSpec((tb, wk), lambda h, i: (i, h)),
            pl.BlockSpec((tb, wv), lambda h, i: (i, v_off + h)),
            pl.BlockSpec((tb, wv), lambda h, i: (i, sg_off + h)),
            pl.BlockSpec((1, dv), lambda h, i: (0, 0)),
            pl.BlockSpec(m2.shape, lambda h, i: (0, 0)),
            pl.BlockSpec(masks.shape, lambda h, i: (0, 0, 0)),
        ],
        out_specs=pl.BlockSpec((tb, wv), lambda h, i: (i, h)),
        out_shape=jax.ShapeDtypeStruct((t, heads * dv), BF16),
        scratch_shapes=[pltpu.VMEM((hb, dv, dk), F32)],
        compiler_params=_cparams(("parallel", "arbitrary")),
        name="recurrence_dv%d" % dv,
    )(q_arr, k_arr, g_arr, v_arr, sg_arr, nw, m2, masks)


def _merge_kernel(ya_ref, yb_ref, wa_ref, wb_ref, ga_ref, gb_ref, o_ref):
    pa = _bdot(ya_ref[...], wa_ref[...].astype(BF16))
    pb = _bdot(yb_ref[...], wb_ref[...].astype(BF16))
    o_ref[...] = (ga_ref[...].astype(F32) * pa + gb_ref[...].astype(F32) * pb).astype(BF16)


def _merge(ya, yb, wa, wb, layer, gates, *, tm, tn):
    t = ya.shape[0]
    nj = D_MODEL // tn
    return pl.pallas_call(
        _merge_kernel,
        grid=(t // tm, nj),
        in_specs=[
            pl.BlockSpec((tm, HGRN_VAL), lambda i, j: (i, 0)),
            pl.BlockSpec((tm, GLA_VAL), lambda i, j: (i, 0)),
            pl.BlockSpec((None, HGRN_VAL, tn), lambda i, j: (layer, 0, j)),
            pl.BlockSpec((None, GLA_VAL, tn), lambda i, j: (layer, 0, j)),
            pl.BlockSpec((tm, tn), lambda i, j: (i, j)),
            pl.BlockSpec((tm, tn), lambda i, j: (i, nj + j)),
        ],
        out_specs=pl.BlockSpec((tm, tn), lambda i, j: (i, j)),
        out_shape=jax.ShapeDtypeStruct((t, D_MODEL), BF16),
        compiler_params=_cparams(("parallel", "arbitrary")),
        name="merge",
    )(ya, yb, wa, wb, gates, gates)


def _outproj_kernel(m_ref, w_ref, x_ref, o_ref):
    o_ref[...] = x_ref[...] + _bdot(m_ref[...], w_ref[...].astype(BF16))


def _outproj(merged, w_out, layer, x, *, tm, tn):
    t = x.shape[0]
    return pl.pallas_call(
        _outproj_kernel,
        grid=(t // tm, D_MODEL // tn),
        in_specs=[
            pl.BlockSpec((tm, D_MODEL), lambda i, j: (i, 0)),
            pl.BlockSpec((None, D_MODEL, tn), lambda i, j: (layer, 0, j)),
            pl.BlockSpec((tm, tn), lambda i, j: (i, j)),
        ],
        out_specs=pl.BlockSpec((tm, tn), lambda i, j: (i, j)),
        out_shape=jax.ShapeDtypeStruct((t, D_MODEL), F32),
        compiler_params=_cparams(("parallel", "arbitrary")),
        name="outproj",
    )(merged, w_out, x)


def _dense_ffn_kernel(x_ref, nw_ref, wg_ref, wu_ref, wd_ref, o_ref, h_scr):
    j = pl.program_id(1)

    @pl.when(j == 0)
    def _():
        x = x_ref[...]
        h_scr[...] = _rms(x, nw_ref[...]).astype(BF16)
        o_ref[...] = x

    h = h_scr[...]
    g = _bdot(h, wg_ref[...].astype(BF16))
    u = _bdot(h, wu_ref[...].astype(BF16))
    a = (g * _sigmoid(g) * u).astype(BF16)
    o_ref[...] += _bdot(a, wd_ref[...].astype(BF16))


def _dense_ffn(x, nw, wg, wu, wd, idx, *, tm, tf):
    t = x.shape[0]
    return pl.pallas_call(
        _dense_ffn_kernel,
        grid=(t // tm, D_FF_DENSE // tf),
        in_specs=[
            pl.BlockSpec((tm, D_MODEL), lambda i, j: (i, 0)),
            pl.BlockSpec((1, D_MODEL), lambda i, j: (0, 0)),
            pl.BlockSpec((None, D_MODEL, tf), lambda i, j: (idx, 0, j)),
            pl.BlockSpec((None, D_MODEL, tf), lambda i, j: (idx, 0, j)),
            pl.BlockSpec((None, tf, D_MODEL), lambda i, j: (idx, j, 0)),
        ],
        out_specs=pl.BlockSpec((tm, D_MODEL), lambda i, j: (i, 0)),
        out_shape=jax.ShapeDtypeStruct((t, D_MODEL), F32),
        scratch_shapes=[pltpu.VMEM((tm, D_MODEL), BF16)],
        compiler_params=_cparams(("parallel", "arbitrary")),
        name="dense_ffn",
    )(x, nw, wg, wu, wd)


def _router_kernel(x_ref, nw_ref, rw_ref, rb_ref, h_ref, idx_ref, wts_ref):
    h = _rms(x_ref[...], nw_ref[...])
    h_ref[...] = h
    logits = jnp.dot(h, rw_ref[...], preferred_element_type=F32,
                     precision=lax.Precision.HIGHEST) + rb_ref[...]
    lane_i = lax.broadcasted_iota(jnp.int32, logits.shape, 1)
    lane = lane_i.astype(F32)
    m1 = jnp.max(logits, axis=-1, keepdims=True)
    i1 = jnp.min(jnp.where(logits == m1, lane, float(LANE)), axis=-1, keepdims=True)
    rest = jnp.where(lane == i1, -jnp.inf, logits)
    m2 = jnp.max(rest, axis=-1, keepdims=True)
    i2 = jnp.min(jnp.where(rest == m2, lane, float(LANE)), axis=-1, keepdims=True)
    e = jnp.exp(m2 - m1)
    w1 = 1.0 / (1.0 + e)
    w2 = e * w1
    idx_ref[...] = jnp.where(lane_i == 0, i1, jnp.where(lane_i == 1, i2, 0.0)).astype(jnp.int32)
    wts_ref[...] = jnp.where(lane_i == 0, w1, jnp.where(lane_i == 1, w2, 0.0))


def _router(x, nw, rw_pad, rb_pad, *, tm):
    t = x.shape[0]
    return pl.pallas_call(
        _router_kernel,
        grid=(t // tm,),
        in_specs=[
            pl.BlockSpec((tm, D_MODEL), lambda i: (i, 0)),
            pl.BlockSpec((1, D_MODEL), lambda i: (0, 0)),
            pl.BlockSpec((D_MODEL, LANE), lambda i: (0, 0)),
            pl.BlockSpec((1, LANE), lambda i: (0, 0)),
        ],
        out_specs=[
            pl.BlockSpec((tm, D_MODEL), lambda i: (i, 0)),
            pl.BlockSpec((tm, LANE), lambda i: (i, 0)),
            pl.BlockSpec((tm, LANE), lambda i: (i, 0)),
        ],
        out_shape=[
            jax.ShapeDtypeStruct((t, D_MODEL), F32),
            jax.ShapeDtypeStruct((t, LANE), jnp.int32),
            jax.ShapeDtypeStruct((t, LANE), F32),
        ],
        compiler_params=_cparams(("parallel",)),
        name="router",
    )(x, nw, rw_pad, rb_pad)


def _dispatch_kernel(dest_ref, pad_start_ref, pad_count_ref, h_ref, xs_hbm, zero_scr, sem, *, nb, n_tok):
    step = pl.program_id(0)
    n_tok_steps = n_tok // nb

    def row_copy(src_ref, src_row, dst_row):
        return pltpu.make_async_copy(src_ref.at[pl.ds(src_row, 1), :],
                                     xs_hbm.at[pl.ds(dst_row, 1), :], sem.at[0])

    @pl.when(step < n_tok_steps)
    def _():
        base = step * (nb * TOP_K)

        def issue(r, c):
            for kk in range(TOP_K):
                row_copy(h_ref, r, dest_ref[base + r * TOP_K + kk]).start()
            return c

        lax.fori_loop(0, nb, issue, 0)

        def drain(r, c):
            for kk in range(TOP_K):
                row_copy(h_ref, 0, 0).wait()
            return c

        lax.fori_loop(0, nb, drain, 0)

    @pl.when(step >= n_tok_steps)
    def _():
        e = step - n_tok_steps
        zero_scr[...] = jnp.zeros_like(zero_scr)
        start = pad_start_ref[e]
        n = pad_count_ref[e]

        def issue(r, c):
            row_copy(zero_scr, 0, start + r).start()
            return c

        lax.fori_loop(0, n, issue, 0)

        def drain(r, c):
            row_copy(zero_scr, 0, 0).wait()
            return c

        lax.fori_loop(0, n, drain, 0)


def _dispatch(dest, pad_start, pad_count, h, t_pad, *, nb):
    n_tok = h.shape[0]
    n_tok_steps = n_tok // nb
    kern = functools.partial(_dispatch_kernel, nb=nb, n_tok=n_tok)
    grid_spec = pltpu.PrefetchScalarGridSpec(
        num_scalar_prefetch=3,
        grid=(n_tok_steps + N_EXPERTS,),
        in_specs=[pl.BlockSpec((nb, D_MODEL),
                               lambda i, d, ps, pc: (jnp.minimum(i, n_tok_steps - 1), 0))],
        out_specs=pl.BlockSpec(memory_space=pl.ANY),
        scratch_shapes=[pltpu.VMEM((8, D_MODEL), F32), pltpu.SemaphoreType.DMA((1,))],
    )
    return pl.pallas_call(
        kern,
        grid_spec=grid_spec,
        out_shape=jax.ShapeDtypeStruct((t_pad, D_MODEL), F32),
        compiler_params=_cparams(("arbitrary",)),
        name="dispatch",
    )(dest, pad_start, pad_count, h)


def _moe_ffn_kernel(te_ref, tv_ref, tb_ref, xs_hbm, wg_ref, wu_ref, wd_ref, ys_hbm,
                    x_scr, acc_scr, wg_scr, wu_scr, wd_scr, pend_ref, sem_in, sem_out, *, sub, tm):
    i = pl.program_id(0)
    j = pl.program_id(1)
    last_i = pl.num_programs(0) - 1
    last_j = pl.num_programs(1) - 1
    valid = tv_ref[i]
    nsub = (valid + sub - 1) // sub
    tile_row0 = tb_ref[i] * tm

    def in_copy(s):
        r = pl.multiple_of(s * sub, sub)
        return pltpu.make_async_copy(xs_hbm.at[pl.ds(tile_row0 + r, sub), :],
                                     x_scr.at[pl.ds(r, sub), :], sem_in.at[0])

    def out_copy(s):
        r = pl.multiple_of(s * sub, sub)
        return pltpu.make_async_copy(acc_scr.at[pl.ds(r, sub), :],
                                     ys_hbm.at[pl.ds(tile_row0 + r, sub), :], sem_out.at[0])

    def wait_pending_out():
        def body(s, c):
            out_copy(0).wait()
            return c

        lax.fori_loop(0, pend_ref[0], body, 0)
        pend_ref[0] = 0

    @pl.when((i == 0) & (j == 0))
    def _():
        pend_ref[0] = 0

    @pl.when(valid > 0)
    def _():
        @pl.when(j == 0)
        def _():
            wait_pending_out()

            def start_in(s, c):
                in_copy(s).start()
                return c

            lax.fori_loop(0, nsub, start_in, 0)

            def zero_rows(s, c):
                acc_scr[pl.ds(pl.multiple_of(s * sub, sub), sub), :] = jnp.zeros((sub, D_MODEL), F32)
                return c

            lax.fori_loop(0, nsub, zero_rows, 0)

            def wait_in(s, c):
                in_copy(0).wait()
                return c

            lax.fori_loop(0, nsub, wait_in, 0)

        wg_scr[...] = wg_ref[...].astype(BF16)
        wu_scr[...] = wu_ref[...].astype(BF16)
        wd_scr[...] = wd_ref[...].astype(BF16)

        def rows_chain(row0):
            rows = pl.ds(row0, sub)
            h = x_scr[rows, :].astype(BF16)
            g = _bdot(h, wg_scr[...])
            u = _bdot(h, wu_scr[...])
            a = (g * _sigmoid(g) * u).astype(BF16)
            acc_scr[rows, :] += _bdot(a, wd_scr[...])

        def pair_body(p, c):
            base = pl.multiple_of(p * (2 * sub), 2 * sub)
            rows_chain(base)
            rows_chain(base + sub)
            return c

        lax.fori_loop(0, nsub // 2, pair_body, 0)

        @pl.when(nsub % 2 == 1)
        def _():
            rows_chain(pl.multiple_of((nsub - 1) * sub, sub))

        @pl.when(j == last_j)
        def _():
            def start_out(s, c):
                out_copy(s).start()
                return c

            lax.fori_loop(0, nsub, start_out, 0)
            pend_ref[0] = nsub

    @pl.when((i == last_i) & (j == last_j))
    def _():
        wait_pending_out()


def _moe_ffn(tile_expert, tile_valid, tile_blk, xs, wg, wu, wd, idx, *, tm, tf, sub):
    t_pad = xs.shape[0]
    n_tiles = t_pad // tm
    nf = D_FF_EXPERT // tf
    kern = functools.partial(_moe_ffn_kernel, sub=sub, tm=tm)

    def jj(i, j, tv):
        return jnp.where(tv[i] > 0, j, nf - 1)

    grid_spec = pltpu.PrefetchScalarGridSpec(
        num_scalar_prefetch=3,
        grid=(n_tiles, nf),
        in_specs=[
            pl.BlockSpec(memory_space=pl.ANY),
            pl.BlockSpec((None, None, D_MODEL, tf), lambda i, j, te, tv, tb: (idx, te[i], 0, jj(i, j, tv))),
            pl.BlockSpec((None, None, D_MODEL, tf), lambda i, j, te, tv, tb: (idx, te[i], 0, jj(i, j, tv))),
            pl.BlockSpec((None, None, tf, D_MODEL), lambda i, j, te, tv, tb: (idx, te[i], jj(i, j, tv), 0)),
        ],
        out_specs=pl.BlockSpec(memory_space=pl.ANY),
        scratch_shapes=[
            pltpu.VMEM((tm, D_MODEL), F32),
            pltpu.VMEM((tm, D_MODEL), F32),
            pltpu.VMEM((D_MODEL, tf), BF16),
            pltpu.VMEM((D_MODEL, tf), BF16),
            pltpu.VMEM((tf, D_MODEL), BF16),
            pltpu.SMEM((1,), jnp.int32),
            pltpu.SemaphoreType.DMA((1,)),
            pltpu.SemaphoreType.DMA((1,)),
        ],
    )
    return pl.pallas_call(
        kern,
        grid_spec=grid_spec,
        out_shape=jax.ShapeDtypeStruct((t_pad, D_MODEL), F32),
        compiler_params=_cparams(("arbitrary", "arbitrary")),
        name="moe_ffn",
    )(tile_expert, tile_valid, tile_blk, xs, wg, wu, wd)


def _combine_kernel(dest_ref, x_ref, wts_ref, fw_ref, ys_hbm, o_ref, ya_scr, yb_scr, sem,
                    *, nb, final_norm):
    base = pl.program_id(0) * nb

    def row_copy(src_row, dst_ref, r):
        return pltpu.make_async_copy(ys_hbm.at[pl.ds(src_row, 1), :],
                                     dst_ref.at[pl.ds(r, 1), :], sem.at[0])

    def issue(r, c):
        s = (base + r) * TOP_K
        row_copy(dest_ref[s], ya_scr, r).start()
        row_copy(dest_ref[s + 1], yb_scr, r).start()
        return c

    lax.fori_loop(0, nb, issue, 0)

    def drain(r, c):
        row_copy(0, ya_scr, 0).wait()
        row_copy(0, yb_scr, 0).wait()
        return c

    lax.fori_loop(0, nb, drain, 0)

    w = wts_ref[...]
    out = x_ref[...] + w[:, 0:1] * ya_scr[...] + w[:, 1:2] * yb_scr[...]
    if final_norm:
        out = _rms(out, fw_ref[...])
    o_ref[...] = out


def _combine(dest, x, wts, fw, ys, *, nb, final_norm):
    t = x.shape[0]
    kern = functools.partial(_combine_kernel, nb=nb, final_norm=final_norm)
    grid_spec = pltpu.PrefetchScalarGridSpec(
        num_scalar_prefetch=1,
        grid=(t // nb,),
        in_specs=[
            pl.BlockSpec((nb, D_MODEL), lambda i, d: (i, 0)),
            pl.BlockSpec((nb, LANE), lambda i, d: (i, 0)),
            pl.BlockSpec((1, D_MODEL), lambda i, d: (0, 0)),
            pl.BlockSpec(memory_space=pl.ANY),
        ],
        out_specs=pl.BlockSpec((nb, D_MODEL), lambda i, d: (i, 0)),
        scratch_shapes=[
            pltpu.VMEM((nb, D_MODEL), F32),
            pltpu.VMEM((nb, D_MODEL), F32),
            pltpu.SemaphoreType.DMA((1,)),
        ],
    )
    return pl.pallas_call(
        kern,
        grid_spec=grid_spec,
        out_shape=jax.ShapeDtypeStruct((t, D_MODEL), F32),
        compiler_params=_cparams(("arbitrary",)),
        name="combine",
    )(dest, x, wts, fw, ys)


def _routing_tables(idx2, tm, sub):
    n_slots = idx2.shape[0] * TOP_K
    e_flat = idx2.reshape(-1)
    onehot = (e_flat[:, None] == jnp.arange(N_EXPERTS, dtype=jnp.int32)[None, :]).astype(jnp.int32)
    csum = jnp.cumsum(onehot, axis=0)
    counts = csum[-1]
    rank = jnp.sum(onehot * csum, axis=1) - 1
    padded = ((counts + tm - 1) // tm) * tm
    gend = jnp.cumsum(padded)
    gstart = gend - padded
    dest = (jnp.sum(onehot * gstart[None, :], axis=1) + rank).astype(jnp.int32)

    n_tiles = n_slots // tm + N_EXPERTS
    t_pad = n_tiles * tm
    tile_row0 = jnp.arange(n_tiles, dtype=jnp.int32) * tm
    n_used = gend[-1] // tm
    tile_blk = jnp.minimum(jnp.arange(n_tiles, dtype=jnp.int32), n_used - 1)
    blk_row0 = tile_blk * tm
    tile_expert = jnp.sum((blk_row0[:, None] >= gend[None, :]).astype(jnp.int32), axis=1)
    tile_expert = jnp.minimum(tile_expert, N_EXPERTS - 1)
    valid = jnp.clip(gstart[tile_expert] + counts[tile_expert] - tile_row0, 0, tm)
    tile_valid = jnp.where(jnp.arange(n_tiles) < n_used, valid, 0).astype(jnp.int32)

    pad_start = (gstart + counts).astype(jnp.int32)
    pad_count = (((counts + sub - 1) // sub) * sub - counts).astype(jnp.int32)
    return (dest, pad_start, pad_count, tile_expert.astype(jnp.int32), tile_valid,
            tile_blk.astype(jnp.int32), t_pad)


def _moe_block(x, nw, rw, rb, wg, wu, wd, idx, fw, final_norm):
    t = x.shape[0]
    tm = min(TM, t)
    rw_pad = jnp.pad(rw, ((0, 0), (0, LANE - N_EXPERTS)))
    rb_pad = jnp.pad(rb, (0, LANE - N_EXPERTS), constant_values=-1e30).reshape(1, LANE)
    moe_tm = min(MOE_TM, TOP_K * t)
    h, idx_l, wts = _router(x, nw, rw_pad, rb_pad, tm=tm)
    dest, pad_start, pad_count, tile_expert, tile_valid, tile_blk, t_pad = _routing_tables(
        idx_l[:, :TOP_K], moe_tm, MOE_SUB)
    xs = _dispatch(dest, pad_start, pad_count, h, t_pad, nb=ROW_BATCH)
    ys = _moe_ffn(tile_expert, tile_valid, tile_blk, xs, wg, wu, wd, idx, tm=moe_tm, tf=TF_MOE,
                  sub=MOE_SUB)
    return _combine(dest, x, wts, fw, ys, nb=ROW_BATCH, final_norm=final_norm)


def _final_norm_kernel(x_ref, w_ref, o_ref):
    o_ref[...] = _rms(x_ref[...], w_ref[...])


def _final_norm(x, w, *, tm):
    t = x.shape[0]
    return pl.pallas_call(
        _final_norm_kernel,
        grid=(t // tm,),
        in_specs=[pl.BlockSpec((tm, D_MODEL), lambda i: (i, 0)),
                  pl.BlockSpec((1, D_MODEL), lambda i: (0, 0))],
        out_specs=pl.BlockSpec((tm, D_MODEL), lambda i: (i, 0)),
        out_shape=jax.ShapeDtypeStruct((t, D_MODEL), F32),
        compiler_params=_cparams(("parallel",)),
        name="final_norm",
    )(x, w)


def _mixer(x, layer, mix_norm_w, w_in, lower_bound_logits, gk_up_w, gk_up_b, hgrn_norm_w, gla_norm_w,
           w_proj_a, w_proj_b, w_out, consts):
    t = x.shape[0]
    tm = min(TM, t)
    tb = min(ATT_TB, t)
    nw = mix_norm_w[layer].reshape(1, D_MODEL)
    bounds = jnp.cumsum(jax.nn.softmax(lower_bound_logits.astype(F32), axis=0), axis=0)
    lb = (bounds[layer] - bounds[0]).reshape(1, HGRN_KEY)
    u, logf = _inproj_main(x, nw, w_in, layer, lb, tm=tm, tn=TN_IN)
    w_tail = w_in[layer, MAIN_COLS:, :]
    w_gk = w_tail[:GLA_GATE_RANK]
    w_m = w_tail[GLA_GATE_RANK:]
    gates, loggk = _inproj_tail(x, nw, w_m, w_gk, gk_up_w[layer], gk_up_b[layer].reshape(1, GLA_KEY),
                                tm=tm, tn=TN_IN)
    hb_a, hb_b = 2, 2
    wa = hb_a * HEAD_K
    y_a = _recurrence(u, 0, u, 1024 // wa, logf, u, 2048 // wa, u, 3072 // wa,
                      hgrn_norm_w[layer].reshape(1, -1), consts,
                      heads=HGRN_HEADS, dv=128, hb=hb_a, tb=tb, c=ATT_CHUNK)
    wkb, wvb = hb_b * HEAD_K, hb_b * 256
    y_b = _recurrence(u, 4096 // wkb, u, 4608 // wkb, loggk, u, 5120 // wvb, u, 6144 // wvb,
                      gla_norm_w[layer].reshape(1, -1), consts,
                      heads=GLA_HEADS, dv=256, hb=hb_b, tb=tb, c=ATT_CHUNK)
    merged = _merge(y_a, y_b, w_proj_a, w_proj_b, layer, gates, tm=tm, tn=TN_OUT)
    return _outproj(merged, w_out, layer, x, tm=tm, tn=TN_OUT)


def kernel(x, mix_norm_w, w_in, lower_bound_logits, gk_up_w, gk_up_b, hgrn_norm_w, gla_norm_w, w_proj_a,
           w_proj_b, w_out, ffn_norm_w, dense_w_gate, dense_w_up, dense_w_down, router_w, router_b,
           moe_w_gate, moe_w_up, moe_w_down, final_norm_w):
    b, t, d = x.shape
    xs = x.reshape(b * t, d)
    consts = _recurrence_constants(ATT_CHUNK)
    fw = final_norm_w.reshape(1, D_MODEL)
    fused_final = False
    w_in_t = jnp.swapaxes(w_in, 1, 2)
    for layer in range(DEPTH):
        xs = _mixer(xs, layer, mix_norm_w, w_in_t, lower_bound_logits, gk_up_w, gk_up_b, hgrn_norm_w,
                    gla_norm_w, w_proj_a, w_proj_b, w_out, consts)
        nw = ffn_norm_w[layer].reshape(1, D_MODEL)
        j = layer // 2
        if layer % 2 == 0:
            xs = _dense_ffn(xs, nw, dense_w_gate, dense_w_up, dense_w_down, j,
                            tm=min(TM, xs.shape[0]), tf=TF_DENSE)
        else:
            fused_final = layer == DEPTH - 1
            xs = _moe_block(xs, nw, router_w[j], router_b[j], moe_w_gate, moe_w_up, moe_w_down, j,
                            fw, fused_final)
    if not fused_final:
        xs = _final_norm(xs, fw, tm=min(TM, xs.shape[0]))
    return xs.reshape(b, t, d)
```

```python
import functools

import numpy as np
import jax
import jax.numpy as jnp
from jax import lax
from jax.experimental import pallas as pl
from jax.experimental.pallas import tpu as pltpu

F32 = jnp.float32
BF16 = jnp.bfloat16

D_MODEL = 2048
DEPTH = 2
HGRN_HEADS = 8
HGRN_KEY = 1024
HGRN_VAL = 1024
GLA_HEADS = 4
GLA_KEY = 512
GLA_VAL = 1024
HEAD_K = 128
GLA_GATE_RANK = 16
GLA_GATE_NORMALIZER = 16.0
D_FF_DENSE = 5632
N_EXPERTS = 8
TOP_K = 2
D_FF_EXPERT = 7168
EPS = 1e-6

MAIN_COLS = 4 * 1024 + 2 * 512 + 2 * 1024
GATE_COLS = 2 * D_MODEL

LANE = 128
VMEM_LIMIT = 56 * 1024 * 1024

TM = 1024
TN_IN = 512
TN_OUT = 512
TF_DENSE = 256
TF_MOE = 256
MOE_TM = 2048
MOE_SUB = 256
ATT_CHUNK = 128
ATT_TB = 1024
ROW_BATCH = 256
ROW_CHUNKS = 4
TM_PROJ = 2048


def _cparams(sem):
    return pltpu.CompilerParams(dimension_semantics=sem, vmem_limit_bytes=VMEM_LIMIT)


def _rms(x, w):
    return x * lax.rsqrt(jnp.mean(x * x, axis=-1, keepdims=True) + EPS) * w


def _sigmoid(x):
    e = jnp.exp(-jnp.abs(x))
    r = 1.0 / (1.0 + e)
    return jnp.where(x >= 0, r, e * r)


def _log_sigmoid(x):
    return jnp.minimum(x, 0.0) - jnp.log1p(jnp.exp(-jnp.abs(x)))


def _bdot(a, b):
    return jnp.dot(a, b, preferred_element_type=F32)


_NT = (((1,), (1,)), ((), ()))
_TN = (((0,), (0,)), ((), ()))


def _bdot_nt(a, b):
    return lax.dot_general(a, b, _NT, preferred_element_type=F32)


def _row_chunks(tm):
    step = max(tm // ROW_CHUNKS, 8)
    return [slice(r, r + step) for r in range(0, tm, step)]


def _single_buffered(block_shape, index_map):
    return pl.BlockSpec(block_shape, index_map, pipeline_mode=pl.Buffered(1))


def _inproj_main_kernel(x_ref, nw_ref, w_ref, lb_ref, u_ref, logf_ref, h_scr, *, tn):
    j = pl.program_id(1)
    chunks = _row_chunks(x_ref.shape[0])

    @pl.when(j == 0)
    def _():
        for rows in chunks:
            h_scr[rows, :] = _rms(x_ref[rows, :], nw_ref[...]).astype(BF16)

    def project(rows):
        return _bdot_nt(h_scr[rows, :], w_ref[...].astype(BF16))

    per = 1024 // tn
    half = 512 // tn
    scale = HEAD_K ** -0.5
    b_f, b_i, b_g = per, 2 * per, 3 * per
    b_qb = 4 * per
    b_kb = b_qb + half
    b_vb = b_kb + half
    b_gb = b_vb + per

    is_q = (j < b_f) | ((j >= b_qb) & (j < b_kb))
    is_f = (j >= b_f) & (j < b_i)
    is_gate = ((j >= b_g) & (j < b_qb)) | (j >= b_gb)
    is_plain = jnp.logical_not(is_q | is_f | is_gate)

    @pl.when(is_q)
    def _():
        for rows in chunks:
            u_ref[rows, :] = (project(rows) * scale).astype(BF16)

    @pl.when(is_plain)
    def _():
        for rows in chunks:
            u_ref[rows, :] = project(rows).astype(BF16)

    @pl.when(is_gate)
    def _():
        for rows in chunks:
            acc = project(rows)
            u_ref[rows, :] = (acc * _sigmoid(acc)).astype(BF16)

    @pl.when(is_f)
    def _():
        lb = lb_ref[...]
        log_lb = jnp.log(lb)
        log_1m_lb = jnp.log1p(-lb)
        for rows in chunks:
            acc = project(rows)
            e = jnp.exp(-jnp.abs(acc))
            r = 1.0 / (1.0 + e)
            sig_neg = jnp.where(acc >= 0, e * r, r)
            ls = jnp.minimum(acc, 0.0) - jnp.log1p(e)
            b = log_1m_lb + ls
            m = jnp.maximum(log_lb, b)
            logf_ref[rows, :] = m + jnp.log1p(jnp.exp(-jnp.abs(log_lb - b)))
            u_ref[rows, :] = ((1.0 - lb) * sig_neg).astype(BF16)


def _inproj_main(x, nw, w_in, layer, lb, *, tm, tn):
    t = x.shape[0]
    per = 1024 // tn
    nj = MAIN_COLS // tn
    kern = functools.partial(_inproj_main_kernel, tn=tn)
    return pl.pallas_call(
        kern,
        grid=(t // tm, nj),
        in_specs=[
            _single_buffered((tm, D_MODEL), lambda i, j: (i, 0)),
            pl.BlockSpec((1, D_MODEL), lambda i, j: (0, 0)),
            pl.BlockSpec((None, tn, D_MODEL), lambda i, j: (layer, j, 0)),
            pl.BlockSpec((1, tn), lambda i, j: (0, jnp.clip(j - per, 0, per - 1))),
        ],
        out_specs=[
            pl.BlockSpec((tm, tn), lambda i, j: (i, j)),
            pl.BlockSpec((tm, tn), lambda i, j: (i, jnp.clip(j - per, 0, per - 1))),
        ],
        out_shape=[
            jax.ShapeDtypeStruct((t, MAIN_COLS), BF16),
            jax.ShapeDtypeStruct((t, HGRN_KEY), F32),
        ],
        scratch_shapes=[pltpu.VMEM((tm, D_MODEL), BF16)],
        compiler_params=_cparams(("parallel", "arbitrary")),
        name="inproj_main",
    )(x, nw, w_in, lb)


def _inproj_tail_kernel(x_ref, nw_ref, wm_ref, wgk_ref, up_ref, upb_ref, gates_ref, loggk_ref, h_scr):
    j = pl.program_id(1)
    chunks = _row_chunks(x_ref.shape[0])

    @pl.when(j == 0)
    def _():
        wgk = wgk_ref[...].astype(BF16)
        up = up_ref[...].astype(BF16)
        for rows in chunks:
            h = _rms(x_ref[rows, :], nw_ref[...]).astype(BF16)
            h_scr[rows, :] = h
            code = _bdot_nt(h, wgk)
            lg = _bdot(code.astype(BF16), up) + upb_ref[...]
            loggk_ref[rows, :] = _log_sigmoid(lg) * (1.0 / GLA_GATE_NORMALIZER)

    for rows in chunks:
        acc = _bdot_nt(h_scr[rows, :], wm_ref[...].astype(BF16))
        gates_ref[rows, :] = _sigmoid(acc).astype(BF16)


def _inproj_tail(x, nw, w_m, w_gk, up_w, up_b, *, tm, tn):
    t = x.shape[0]
    return pl.pallas_call(
        _inproj_tail_kernel,
        grid=(t // tm, GATE_COLS // tn),
        in_specs=[
            _single_buffered((tm, D_MODEL), lambda i, j: (i, 0)),
            pl.BlockSpec((1, D_MODEL), lambda i, j: (0, 0)),
            pl.BlockSpec((tn, D_MODEL), lambda i, j: (j, 0)),
            pl.BlockSpec((GLA_GATE_RANK, D_MODEL), lambda i, j: (0, 0)),
            pl.BlockSpec((GLA_GATE_RANK, GLA_KEY), lambda i, j: (0, 0)),
            pl.BlockSpec((1, GLA_KEY), lambda i, j: (0, 0)),
        ],
        out_specs=[
            pl.BlockSpec((tm, tn), lambda i, j: (i, j)),
            pl.BlockSpec((tm, GLA_KEY), lambda i, j: (i, 0)),
        ],
        out_shape=[
            jax.ShapeDtypeStruct((t, GATE_COLS), BF16),
            jax.ShapeDtypeStruct((t, GLA_KEY), F32),
        ],
        scratch_shapes=[pltpu.VMEM((tm, D_MODEL), BF16)],
        compiler_params=_cparams(("parallel", "arbitrary")),
        name="inproj_tail",
    )(x, nw, w_m, w_gk, up_w, up_b)


def _recurrence_constants(c):
    levels = int(np.log2(c))
    t = np.arange(c)[:, None]
    s = np.arange(c)[None, :]
    mats = [(s <= t), (s > t)]
    masks = [np.eye(c, dtype=bool)]
    for l in range(levels):
        b = 1 << l
        tb = (t // (2 * b)) * 2 * b + b - 1
        is_i = (t // b) % 2 == 1
        mats.append(np.where(is_i, (s > tb) & (s <= t), (s > t) & (s <= tb)))
        i = np.arange(c)[:, None]
        j = np.arange(c)[None, :]
        masks.append((i // (2 * b) == j // (2 * b)) & ((i // b) % 2 == 1) & ((j // b) % 2 == 0))
    m = np.concatenate(mats, axis=0).astype(np.float32)
    m2 = np.concatenate([m, m], axis=1)
    return jnp.asarray(m2, BF16), jnp.asarray(np.stack(masks).astype(np.float32))


def _recurrence_kernel(q_ref, k_ref, g_ref, v_ref, sg_ref, nw_ref, m2_ref, masks_ref, y_ref, st_ref,
                       *, c, hb, dk, dv, nchunk):
    levels = int(np.log2(c))

    @pl.when(pl.program_id(1) == 0)
    def _():
        st_ref[...] = jnp.zeros_like(st_ref)

    row = lax.broadcasted_iota(jnp.int32, (c, dk), 0)
    nw = nw_ref[...]

    def chunk_body(ci, carry):
        r0 = pl.multiple_of(ci * c, c)
        g = g_ref[pl.ds(r0, c), :]
        g_hi = g.astype(BF16)
        g_lo = (g - g_hi.astype(F32)).astype(BF16)
        d_grp = _bdot(m2_ref[...], jnp.concatenate([g_hi, g_lo], axis=0))
        e_grp = jnp.exp(d_grp)
        for hh in range(hb):
            ks = slice(hh * dk, (hh + 1) * dk)
            vs = slice(hh * dv, (hh + 1) * dv)
            q = q_ref[pl.ds(r0, c), ks]
            k = k_ref[pl.ds(r0, c), ks]
            v = v_ref[pl.ds(r0, c), vs]
            e_all = e_grp[:, ks]
            qf = q.astype(F32)
            kf = k.astype(F32)
            e_cum = e_all[0:c]
            q_in = (qf * e_cum).astype(BF16)
            k_out = (kf * e_all[c:2 * c]).astype(BF16)
            scores = lax.dot_general(q, k, _NT, preferred_element_type=F32) * masks_ref[0]
            for l in range(levels):
                b = 1 << l
                if b >= 8:
                    parts = [(qf if (r // b) % 2 == 1 else kf)[r:r + b] for r in range(0, c, b)]
                    qk = jnp.concatenate(parts, axis=0)
                else:
                    qk = jnp.where(((row >> l) & 1) == 1, qf, kf)
                x = (qk * e_all[(l + 2) * c:(l + 3) * c]).astype(BF16)
                scores = scores + lax.dot_general(x, x, _NT, preferred_element_type=F32) * masks_ref[l + 1]
            st = st_ref[hh]
            o = _bdot(scores.astype(BF16), v) + lax.dot_general(
                q_in, st.astype(BF16), _NT, preferred_element_type=F32)
            st_ref[hh] = st * e_cum[c - 1:c, :] + lax.dot_general(
                v, k_out, _TN, preferred_element_type=F32)
            o = o * lax.rsqrt(jnp.mean(o * o, axis=-1, keepdims=True) + EPS) * nw
            y_ref[pl.ds(r0, c), vs] = (o * sg_ref[pl.ds(r0, c), vs].astype(F32)).astype(BF16)
        return carry

    lax.fori_loop(0, nchunk, chunk_body, 0)


def _recurrence(q_arr, q_off, k_arr, k_off, g_arr, v_arr, v_off, sg_arr, sg_off, nw, consts,
                *, heads, dv, hb, tb, c):
    t = q_arr.shape[0]
    dk = HEAD_K
    m2, masks = consts
    kern = functools.partial(_recurrence_kernel, c=c, hb=hb, dk=dk, dv=dv, nchunk=tb // c)
    wk, wv = hb * dk, hb * dv
    return pl.pallas_call(
        kern,
        grid=(heads // hb, t // tb),
        in_specs=[
            pl.BlockSpec((tb, wk), lambda h, i: (i, q_off + h)),
            pl.BlockSpec((tb, wk), lambda h, i: (i, k_off + h)),
            pl.BlockSpec((tb, wk), lambda h, i: (i, h)),
            pl.BlockSpec((tb, wv), lambda h, i: (i, v_off + h)),
            pl.BlockSpec((tb, wv), lambda h, i: (i, sg_off + h)),
            pl.BlockSpec((1, dv), lambda h, i: (0, 0)),
            pl.BlockSpec(m2.shape, lambda h, i: (0, 0)),
            pl.BlockSpec(masks.shape, lambda h, i: (0, 0, 0)),
        ],
        out_specs=pl.BlockSpec((tb, wv), lambda h, i: (i, h)),
        out_shape=jax.ShapeDtypeStruct((t, heads * dv), BF16),
        scratch_shapes=[pltpu.VMEM((hb, dv, dk), F32)],
        compiler_params=_cparams(("parallel", "arbitrary")),
        name="recurrence_dv%d" % dv,
    )(q_arr, k_arr, g_arr, v_arr, sg_arr, nw, m2, masks)


def _merge_kernel(ya_ref, yb_ref, wa_ref, wb_ref, ga_ref, gb_ref, o_ref):
    pa = _bdot(ya_ref[...], wa_ref[...].astype(BF16))
    pb = _bdot(yb_ref[...], wb_ref[...].astype(BF16))
    o_ref[...] = (ga_ref[...].astype(F32) * pa + gb_ref[...].astype(F32) * pb).astype(BF16)


def _merge(ya, yb, wa, wb, layer, gates, *, tm, tn):
    t = ya.shape[0]
    nj = D_MODEL // tn
    return pl.pallas_call(
        _merge_kernel,
        grid=(t // tm, nj),
        in_specs=[
            pl.BlockSpec((tm, HGRN_VAL), lambda i, j: (i, 0)),
            pl.BlockSpec((tm, GLA_VAL), lambda i, j: (i, 0)),
            pl.BlockSpec((None, HGRN_VAL, tn), lambda i, j: (layer, 0, j)),
            pl.BlockSpec((None, GLA_VAL, tn), lambda i, j: (layer, 0, j)),
            pl.BlockSpec((tm, tn), lambda i, j: (i, j)),
            pl.BlockSpec((tm, tn), lambda i, j: (i, nj + j)),
        ],
        out_specs=pl.BlockSpec((tm, tn), lambda i, j: (i, j)),
        out_shape=jax.ShapeDtypeStruct((t, D_MODEL), BF16),
        compiler_params=_cparams(("parallel", "arbitrary")),
        name="merge",
    )(ya, yb, wa, wb, gates, gates)


def _outproj_kernel(m_ref, w_ref, x_ref, o_ref):
    o_ref[...] = x_ref[...] + _bdot(m_ref[...], w_ref[...].astype(BF16))


def _outproj(merged, w_out, layer, x, *, tm, tn):
    t = x.shape[0]
    return pl.pallas_call(
        _outproj_kernel,
        grid=(t // tm, D_MODEL // tn),
        in_specs=[
            pl.BlockSpec((tm, D_MODEL), lambda i, j: (i, 0)),
            pl.BlockSpec((None, D_MODEL, tn), lambda i, j: (layer, 0, j)),
            pl.BlockSpec((tm, tn), lambda i, j: (i, j)),
        ],
        out_specs=pl.BlockSpec((tm, tn), lambda i, j: (i, j)),
        out_shape=jax.ShapeDtypeStruct((t, D_MODEL), F32),
        compiler_params=_cparams(("parallel", "arbitrary")),
        name="outproj",
    )(merged, w_out, x)


def _dense_ffn_kernel(x_ref, nw_ref, wg_ref, wu_ref, wd_ref, o_ref, h_scr):
    j = pl.program_id(1)

    @pl.when(j == 0)
    def _():
        x = x_ref[...]
        h_scr[...] = _rms(x, nw_ref[...]).astype(BF16)
        o_ref[...] = x

    h = h_scr[...]
    g = _bdot(h, wg_ref[...].astype(BF16))
    u = _bdot(h, wu_ref[...].astype(BF16))
    a = (g * _sigmoid(g) * u).astype(BF16)
    o_ref[...] += _bdot(a, wd_ref[...].astype(BF16))


def _dense_ffn(x, nw, wg, wu, wd, idx, *, tm, tf):
    t = x.shape[0]
    return pl.pallas_call(
        _dense_ffn_kernel,
        grid=(t // tm, D_FF_DENSE // tf),
        in_specs=[
            pl.BlockSpec((tm, D_MODEL), lambda i, j: (i, 0)),
            pl.BlockSpec((1, D_MODEL), lambda i, j: (0, 0)),
            pl.BlockSpec((None, D_MODEL, tf), lambda i, j: (idx, 0, j)),
            pl.BlockSpec((None, D_MODEL, tf), lambda i, j: (idx, 0, j)),
            pl.BlockSpec((None, tf, D_MODEL), lambda i, j: (idx, j, 0)),
        ],
        out_specs=pl.BlockSpec((tm, D_MODEL), lambda i, j: (i, 0)),
        out_shape=jax.ShapeDtypeStruct((t, D_MODEL), F32),
        scratch_shapes=[pltpu.VMEM((tm, D_MODEL), BF16)],
        compiler_params=_cparams(("parallel", "arbitrary")),
        name="dense_ffn",
    )(x, nw, wg, wu, wd)


def _router_kernel(x_ref, nw_ref, rw_ref, rb_ref, h_ref, idx_ref, wts_ref):
    h = _rms(x_ref[...], nw_ref[...])
    h_ref[...] = h
    logits = jnp.dot(h, rw_ref[...], preferred_element_type=F32,
                     precision=lax.Precision.HIGHEST) + rb_ref[...]
    lane_i = lax.broadcasted_iota(jnp.int32, logits.shape, 1)
    lane = lane_i.astype(F32)
    m1 = jnp.max(logits, axis=-1, keepdims=True)
    i1 = jnp.min(jnp.where(logits == m1, lane, float(LANE)), axis=-1, keepdims=True)
    rest = jnp.where(lane == i1, -jnp.inf, logits)
    m2 = jnp.max(rest, axis=-1, keepdims=True)
    i2 = jnp.min(jnp.where(rest == m2, lane, float(LANE)), axis=-1, keepdims=True)
    e = jnp.exp(m2 - m1)
    w1 = 1.0 / (1.0 + e)
    w2 = e * w1
    idx_ref[...] = jnp.where(lane_i == 0, i1, jnp.where(lane_i == 1, i2, 0.0)).astype(jnp.int32)
    wts_ref[...] = jnp.where(lane_i == 0, w1, jnp.where(lane_i == 1, w2, 0.0))


def _router(x, nw, rw_pad, rb_pad, *, tm):
    t = x.shape[0]
    return pl.pallas_call(
        _router_kernel,
        grid=(t // tm,),
        in_specs=[
            pl.BlockSpec((tm, D_MODEL), lambda i: (i, 0)),
            pl.BlockSpec((1, D_MODEL), lambda i: (0, 0)),
            pl.BlockSpec((D_MODEL, LANE), lambda i: (0, 0)),
            pl.BlockSpec((1, LANE), lambda i: (0, 0)),
        ],
        out_specs=[
            pl.BlockSpec((tm, D_MODEL), lambda i: (i, 0)),
            pl.BlockSpec((tm, LANE), lambda i: (i, 0)),
            pl.BlockSpec((tm, LANE), lambda i: (i, 0)),
        ],
        out_shape=[
            jax.ShapeDtypeStruct((t, D_MODEL), F32),
            jax.ShapeDtypeStruct((t, LANE), jnp.int32),
            jax.ShapeDtypeStruct((t, LANE), F32),
        ],
        compiler_params=_cparams(("parallel",)),
        name="router",
    )(x, nw, rw_pad, rb_pad)


def _dispatch_kernel(dest_ref, pad_start_ref, pad_count_ref, h_ref, xs_hbm, zero_scr, sem, *, nb, n_tok):
    step = pl.program_id(0)
    n_tok_steps = n_tok // nb

    def row_copy(src_ref, src_row, dst_row):
        return pltpu.make_async_copy(src_ref.at[pl.ds(src_row, 1), :],
                                     xs_hbm.at[pl.ds(dst_row, 1), :], sem.at[0])

    @pl.when(step < n_tok_steps)
    def _():
        base = step * (nb * TOP_K)

        def issue(r, c):
            for kk in range(TOP_K):
                row_copy(h_ref, r, dest_ref[base + r * TOP_K + kk]).start()
            return c

        lax.fori_loop(0, nb, issue, 0)

        def drain(r, c):
            for kk in range(TOP_K):
                row_copy(h_ref, 0, 0).wait()
            return c

        lax.fori_loop(0, nb, drain, 0)

    @pl.when(step >= n_tok_steps)
    def _():
        e = step - n_tok_steps
        zero_scr[...] = jnp.zeros_like(zero_scr)
        start = pad_start_ref[e]
        n = pad_count_ref[e]

        def issue(r, c):
            row_copy(zero_scr, 0, start + r).start()
            return c

        lax.fori_loop(0, n, issue, 0)

        def drain(r, c):
            row_copy(zero_scr, 0, 0).wait()
            return c

        lax.fori_loop(0, n, drain, 0)


def _dispatch(dest, pad_start, pad_count, h, t_pad, *, nb):
    n_tok = h.shape[0]
    n_tok_steps = n_tok // nb
    kern = functools.partial(_dispatch_kernel, nb=nb, n_tok=n_tok)
    grid_spec = pltpu.PrefetchScalarGridSpec(
        num_scalar_prefetch=3,
        grid=(n_tok_steps + N_EXPERTS,),
        in_specs=[pl.BlockSpec((nb, D_MODEL),
                               lambda i, d, ps, pc: (jnp.minimum(i, n_tok_steps - 1), 0))],
        out_specs=pl.BlockSpec(memory_space=pl.ANY),
        scratch_shapes=[pltpu.VMEM((8, D_MODEL), F32), pltpu.SemaphoreType.DMA((1,))],
    )
    return pl.pallas_call(
        kern,
        grid_spec=grid_spec,
        out_shape=jax.ShapeDtypeStruct((t_pad, D_MODEL), F32),
        compiler_params=_cparams(("arbitrary",)),
        name="dispatch",
    )(dest, pad_start, pad_count, h)


def _moe_ffn_kernel(te_ref, tv_ref, tb_ref, xs_hbm, wg_ref, wu_ref, wd_ref, ys_hbm,
                    x_scr, acc_scr, wg_scr, wu_scr, wd_scr, a_scr, pend_ref, sem_in, sem_out, *, sub, tm):
    i = pl.program_id(0)
    j = pl.program_id(1)
    last_i = pl.num_programs(0) - 1
    last_j = pl.num_programs(1) - 1
    valid = tv_ref[i]
    nsub = (valid + sub - 1) // sub
    tile_row0 = tb_ref[i] * tm

    def in_copy(s):
        r = pl.multiple_of(s * sub, sub)
        return pltpu.make_async_copy(xs_hbm.at[pl.ds(tile_row0 + r, sub), :],
                                     x_scr.at[pl.ds(r, sub), :], sem_in.at[0])

    def out_copy(s):
        r = pl.multiple_of(s * sub, sub)
        return pltpu.make_async_copy(acc_scr.at[pl.ds(r, sub), :],
                                     ys_hbm.at[pl.ds(tile_row0 + r, sub), :], sem_out.at[0])

    def wait_pending_out():
        def body(s, c):
            out_copy(0).wait()
            return c

        lax.fori_loop(0, pend_ref[0], body, 0)
        pend_ref[0] = 0

    @pl.when((i == 0) & (j == 0))
    def _():
        pend_ref[0] = 0

    @pl.when(valid > 0)
    def _():
        @pl.when(j == 0)
        def _():
            wait_pending_out()

            def start_in(s, c):
                in_copy(s).start()
                return c

            lax.fori_loop(0, nsub, start_in, 0)

            def zero_rows(s, c):
                acc_scr[pl.ds(pl.multiple_of(s * sub, sub), sub), :] = jnp.zeros((sub, D_MODEL), F32)
                return c

            lax.fori_loop(0, nsub, zero_rows, 0)

            def wait_in(s, c):
                in_copy(0).wait()
                return c

            lax.fori_loop(0, nsub, wait_in, 0)

        wg_scr[...] = wg_ref[...].astype(BF16)
        wu_scr[...] = wu_ref[...].astype(BF16)
        wd_scr[...] = wd_ref[...].astype(BF16)

        def up(row0):
            h = x_scr[pl.ds(row0, sub), :].astype(BF16)
            g = _bdot(h, wg_scr[...])
            u = _bdot(h, wu_scr[...])
            return (g * _sigmoid(g) * u).astype(BF16)

        def down(row0, a):
            acc_scr[pl.ds(row0, sub), :] += _bdot(a, wd_scr[...])

        npair = nsub // 2
        blk = 2 * sub

        @pl.when(npair > 0)
        def _():
            a_scr[0] = up(0)
            a_scr[1] = up(sub)

            def pair_body(p, c):
                base = pl.multiple_of(p * blk, blk)
                prev0 = a_scr[0]
                prev1 = a_scr[1]
                next0 = up(base)
                next1 = up(base + sub)
                down(base - blk, prev0)
                down(base - sub, prev1)
                a_scr[0] = next0
                a_scr[1] = next1
                return c

            lax.fori_loop(1, npair, pair_body, 0)
            tail = pl.multiple_of((npair - 1) * blk, blk)
            down(tail, a_scr[0])
            down(tail + sub, a_scr[1])

        @pl.when(nsub % 2 == 1)
        def _():
            last = pl.multiple_of((nsub - 1) * sub, sub)
            down(last, up(last))

        @pl.when(j == last_j)
        def _():
            def start_out(s, c):
                out_copy(s).start()
                return c

            lax.fori_loop(0, nsub, start_out, 0)
            pend_ref[0] = nsub

    @pl.when((i == last_i) & (j == last_j))
    def _():
        wait_pending_out()


def _moe_ffn(tile_expert, tile_valid, tile_blk, xs, wg, wu, wd, idx, *, tm, tf, sub):
    t_pad = xs.shape[0]
    n_tiles = t_pad // tm
    nf = D_FF_EXPERT // tf
    kern = functools.partial(_moe_ffn_kernel, sub=sub, tm=tm)

    def jj(i, j, tv):
        return jnp.where(tv[i] > 0, j, nf - 1)

    grid_spec = pltpu.PrefetchScalarGridSpec(
        num_scalar_prefetch=3,
        grid=(n_tiles, nf),
        in_specs=[
            pl.BlockSpec(memory_space=pl.ANY),
            pl.BlockSpec((None, None, D_MODEL, tf), lambda i, j, te, tv, tb: (idx, te[i], 0, jj(i, j, tv))),
            pl.BlockSpec((None, None, D_MODEL, tf), lambda i, j, te, tv, tb: (idx, te[i], 0, jj(i, j, tv))),
            pl.BlockSpec((None, None, tf, D_MODEL), lambda i, j, te, tv, tb: (idx, te[i], jj(i, j, tv), 0)),
        ],
        out_specs=pl.BlockSpec(memory_space=pl.ANY),
        scratch_shapes=[
            pltpu.VMEM((tm, D_MODEL), F32),
            pltpu.VMEM((tm, D_MODEL), F32),
            pltpu.VMEM((D_MODEL, tf), BF16),
            pltpu.VMEM((D_MODEL, tf), BF16),
            pltpu.VMEM((tf, D_MODEL), BF16),
            pltpu.VMEM((2, sub, tf), BF16),
            pltpu.SMEM((1,), jnp.int32),
            pltpu.SemaphoreType.DMA((1,)),
            pltpu.SemaphoreType.DMA((1,)),
        ],
    )
    return pl.pallas_call(
        kern,
        grid_spec=grid_spec,
        out_shape=jax.ShapeDtypeStruct((t_pad, D_MODEL), F32),
        compiler_params=_cparams(("arbitrary", "arbitrary")),
        name="moe_ffn",
    )(tile_expert, tile_valid, tile_blk, xs, wg, wu, wd)


def _combine_kernel(dest_ref, x_ref, wts_ref, fw_ref, ys_hbm, o_ref, ya_scr, yb_scr, sem,
                    *, nb, final_norm):
    base = pl.program_id(0) * nb

    def row_copy(src_row, dst_ref, r):
        return pltpu.make_async_copy(ys_hbm.at[pl.ds(src_row, 1), :],
                                     dst_ref.at[pl.ds(r, 1), :], sem.at[0])

    def issue(r, c):
        s = (base + r) * TOP_K
        row_copy(dest_ref[s], ya_scr, r).start()
        row_copy(dest_ref[s + 1], yb_scr, r).start()
        return c

    lax.fori_loop(0, nb, issue, 0)

    def drain(r, c):
        row_copy(0, ya_scr, 0).wait()
        row_copy(0, yb_scr, 0).wait()
        return c

    lax.fori_loop(0, nb, drain, 0)

    w = wts_ref[...]
    out = x_ref[...] + w[:, 0:1] * ya_scr[...] + w[:, 1:2] * yb_scr[...]
    if final_norm:
        out = _rms(out, fw_ref[...])
    o_ref[...] = out


def _combine(dest, x, wts, fw, ys, *, nb, final_norm):
    t = x.shape[0]
    kern = functools.partial(_combine_kernel, nb=nb, final_norm=final_norm)
    grid_spec = pltpu.PrefetchScalarGridSpec(
        num_scalar_prefetch=1,
        grid=(t // nb,),
        in_specs=[
            pl.BlockSpec((nb, D_MODEL), lambda i, d: (i, 0)),
            pl.BlockSpec((nb, LANE), lambda i, d: (i, 0)),
            pl.BlockSpec((1, D_MODEL), lambda i, d: (0, 0)),
            pl.BlockSpec(memory_space=pl.ANY),
        ],
        out_specs=pl.BlockSpec((nb, D_MODEL), lambda i, d: (i, 0)),
        scratch_shapes=[
            pltpu.VMEM((nb, D_MODEL), F32),
            pltpu.VMEM((nb, D_MODEL), F32),
            pltpu.SemaphoreType.DMA((1,)),
        ],
    )
    return pl.pallas_call(
        kern,
        grid_spec=grid_spec,
        out_shape=jax.ShapeDtypeStruct((t, D_MODEL), F32),
        compiler_params=_cparams(("arbitrary",)),
        name="combine",
    )(dest, x, wts, fw, ys)


def _routing_tables(idx2, tm, sub):
    n_slots = idx2.shape[0] * TOP_K
    e_flat = idx2.reshape(-1)
    onehot = (e_flat[:, None] == jnp.arange(N_EXPERTS, dtype=jnp.int32)[None, :]).astype(jnp.int32)
    csum = jnp.cumsum(onehot, axis=0)
    counts = csum[-1]
    rank = jnp.sum(onehot * csum, axis=1) - 1
    padded = ((counts + tm - 1) // tm) * tm
    gend = jnp.cumsum(padded)
    gstart = gend - padded
    dest = (jnp.sum(onehot * gstart[None, :], axis=1) + rank).astype(jnp.int32)

    n_tiles = n_slots // tm + N_EXPERTS
    t_pad = n_tiles * tm
    tile_row0 = jnp.arange(n_tiles, dtype=jnp.int32) * tm
    n_used = gend[-1] // tm
    tile_blk = jnp.minimum(jnp.arange(n_tiles, dtype=jnp.int32), n_used - 1)
    blk_row0 = tile_blk * tm
    tile_expert = jnp.sum((blk_row0[:, None] >= gend[None, :]).astype(jnp.int32), axis=1)
    tile_expert = jnp.minimum(tile_expert, N_EXPERTS - 1)
    valid = jnp.clip(gstart[tile_expert] + counts[tile_expert] - tile_row0, 0, tm)
    tile_valid = jnp.where(jnp.arange(n_tiles) < n_used, valid, 0).astype(jnp.int32)

    pad_start = (gstart + counts).astype(jnp.int32)
    pad_count = (((counts + sub - 1) // sub) * sub - counts).astype(jnp.int32)
    return (dest, pad_start, pad_count, tile_expert.astype(jnp.int32), tile_valid,
            tile_blk.astype(jnp.int32), t_pad)


def _moe_block(x, nw, rw, rb, wg, wu, wd, idx, fw, final_norm):
    t = x.shape[0]
    tm = min(TM, t)
    rw_pad = jnp.pad(rw, ((0, 0), (0, LANE - N_EXPERTS)))
    rb_pad = jnp.pad(rb, (0, LANE - N_EXPERTS), constant_values=-1e30).reshape(1, LANE)
    moe_tm = min(MOE_TM, TOP_K * t)
    h, idx_l, wts = _router(x, nw, rw_pad, rb_pad, tm=tm)
    dest, pad_start, pad_count, tile_expert, tile_valid, tile_blk, t_pad = _routing_tables(
        idx_l[:, :TOP_K], moe_tm, MOE_SUB)
    xs = _dispatch(dest, pad_start, pad_count, h, t_pad, nb=ROW_BATCH)
    ys = _moe_ffn(tile_expert, tile_valid, tile_blk, xs, wg, wu, wd, idx, tm=moe_tm, tf=TF_MOE,
                  sub=MOE_SUB)
    return _combine(dest, x, wts, fw, ys, nb=ROW_BATCH, final_norm=final_norm)


def _final_norm_kernel(x_ref, w_ref, o_ref):
    o_ref[...] = _rms(x_ref[...], w_ref[...])


def _final_norm(x, w, *, tm):
    t = x.shape[0]
    return pl.pallas_call(
        _final_norm_kernel,
        grid=(t // tm,),
        in_specs=[pl.BlockSpec((tm, D_MODEL), lambda i: (i, 0)),
                  pl.BlockSpec((1, D_MODEL), lambda i: (0, 0))],
        out_specs=pl.BlockSpec((tm, D_MODEL), lambda i: (i, 0)),
        out_shape=jax.ShapeDtypeStruct((t, D_MODEL), F32),
        compiler_params=_cparams(("parallel",)),
        name="final_norm",
    )(x, w)


def _mixer(x, layer, mix_norm_w, w_in, lower_bound_logits, gk_up_w, gk_up_b, hgrn_norm_w, gla_norm_w,
           w_proj_a, w_proj_b, w_out, consts):
    t = x.shape[0]
    tm = min(TM_PROJ, t)
    tb = min(ATT_TB, t)
    nw = mix_norm_w[layer].reshape(1, D_MODEL)
    bounds = jnp.cumsum(jax.nn.softmax(lower_bound_logits.astype(F32), axis=0), axis=0)
    lb = (bounds[layer] - bounds[0]).reshape(1, HGRN_KEY)
    u, logf = _inproj_main(x, nw, w_in, layer, lb, tm=tm, tn=TN_IN)
    w_tail = w_in[layer, MAIN_COLS:, :]
    w_gk = w_tail[:GLA_GATE_RANK]
    w_m = w_tail[GLA_GATE_RANK:]
    gates, loggk = _inproj_tail(x, nw, w_m, w_gk, gk_up_w[layer], gk_up_b[layer].reshape(1, GLA_KEY),
                                tm=tm, tn=TN_IN)
    hb_a, hb_b = 2, 2
    wa = hb_a * HEAD_K
    y_a = _recurrence(u, 0, u, 1024 // wa, logf, u, 2048 // wa, u, 3072 // wa,
                      hgrn_norm_w[layer].reshape(1, -1), consts,
                      heads=HGRN_HEADS, dv=128, hb=hb_a, tb=tb, c=ATT_CHUNK)
    wkb, wvb = hb_b * HEAD_K, hb_b * 256
    y_b = _recurrence(u, 4096 // wkb, u, 4608 // wkb, loggk, u, 5120 // wvb, u, 6144 // wvb,
                      gla_norm_w[layer].reshape(1, -1), consts,
                      heads=GLA_HEADS, dv=256, hb=hb_b, tb=tb, c=ATT_CHUNK)
    merged = _merge(y_a, y_b, w_proj_a, w_proj_b, layer, gates, tm=tm, tn=TN_OUT)
    return _outproj(merged, w_out, layer, x, tm=tm, tn=TN_OUT)


def kernel(x, mix_norm_w, w_in, lower_bound_logits, gk_up_w, gk_up_b, hgrn_norm_w, gla_norm_w, w_proj_a,
           w_proj_b, w_out, ffn_norm_w, dense_w_gate, dense_w_up, dense_w_down, router_w, router_b,
           moe_w_gate, moe_w_up, moe_w_down, final_norm_w):
    b, t, d = x.shape
    xs = x.reshape(b * t, d)
    consts = _recurrence_constants(ATT_CHUNK)
    fw = final_norm_w.reshape(1, D_MODEL)
    fused_final = False
    w_in_t = jnp.swapaxes(w_in, 1, 2)
    for layer in range(DEPTH):
        xs = _mixer(xs, layer, mix_norm_w, w_in_t, lower_bound_logits, gk_up_w, gk_up_b, hgrn_norm_w,
                    gla_norm_w, w_proj_a, w_proj_b, w_out, consts)
        nw = ffn_norm_w[layer].reshape(1, D_MODEL)
        j = layer // 2
        if layer % 2 == 0:
            xs = _dense_ffn(xs, nw, dense_w_gate, dense_w_up, dense_w_down, j,
                            tm=min(TM, xs.shape[0]), tf=TF_DENSE)
        else:
            fused_final = layer == DEPTH - 1
            xs = _moe_block(xs, nw, router_w[j], router_b[j], moe_w_gate, moe_w_up, moe_w_down, j,
                            fw, fused_final)
    if not fused_final:
        xs = _final_norm(xs, fw, tm=min(TM, xs.shape[0]))
    return xs.reshape(b, t, d)
```

```python
import functools

import numpy as np
import jax
import jax.numpy as jnp
from jax import lax
from jax.experimental import pallas as pl
from jax.experimental.pallas import tpu as pltpu

F32 = jnp.float32
BF16 = jnp.bfloat16

D_MODEL = 2048
DEPTH = 2
HGRN_HEADS = 8
HGRN_KEY = 1024
HGRN_VAL = 1024
GLA_HEADS = 4
GLA_KEY = 512
GLA_VAL = 1024
HEAD_K = 128
GLA_GATE_RANK = 16
GLA_GATE_NORMALIZER = 16.0
D_FF_DENSE = 5632
N_EXPERTS = 8
TOP_K = 2
D_FF_EXPERT = 7168
EPS = 1e-6

MAIN_COLS = 4 * 1024 + 2 * 512 + 2 * 1024
GATE_COLS = 2 * D_MODEL

LANE = 128
VMEM_LIMIT = 56 * 1024 * 1024

TM = 1024
TN_IN = 512
TN_OUT = 512
TF_DENSE = 256
TF_MOE = 256
MOE_TM = 2048
MOE_SUB = 256
ATT_CHUNK = 128
ATT_TB = 1024
ROW_BATCH = 256
ROW_CHUNKS = 4
TM_PROJ = 2048


def _cparams(sem):
    return pltpu.CompilerParams(dimension_semantics=sem, vmem_limit_bytes=VMEM_LIMIT)


def _rms(x, w):
    return x * lax.rsqrt(jnp.mean(x * x, axis=-1, keepdims=True) + EPS) * w


def _sigmoid(x):
    e = jnp.exp(-jnp.abs(x))
    r = 1.0 / (1.0 + e)
    return jnp.where(x >= 0, r, e * r)


def _log_sigmoid(x):
    return jnp.minimum(x, 0.0) - jnp.log1p(jnp.exp(-jnp.abs(x)))


def _bdot(a, b):
    return jnp.dot(a, b, preferred_element_type=F32)


_NT = (((1,), (1,)), ((), ()))
_TN = (((0,), (0,)), ((), ()))


def _bdot_nt(a, b):
    return lax.dot_general(a, b, _NT, preferred_element_type=F32)


def _row_chunks(tm):
    step = max(tm // ROW_CHUNKS, 8)
    return [slice(r, r + step) for r in range(0, tm, step)]


def _single_buffered(block_shape, index_map):
    return pl.BlockSpec(block_shape, index_map, pipeline_mode=pl.Buffered(1))


def _inproj_main_kernel(x_ref, nw_ref, w_ref, lb_ref, u_ref, logf_ref, h_scr, *, tn):
    j = pl.program_id(1)
    chunks = _row_chunks(x_ref.shape[0])

    @pl.when(j == 0)
    def _():
        for rows in chunks:
            h_scr[rows, :] = _rms(x_ref[rows, :], nw_ref[...]).astype(BF16)

    def project(rows):
        return _bdot_nt(h_scr[rows, :], w_ref[...].astype(BF16))

    per = 1024 // tn
    half = 512 // tn
    scale = HEAD_K ** -0.5
    b_f, b_i, b_g = per, 2 * per, 3 * per
    b_qb = 4 * per
    b_kb = b_qb + half
    b_vb = b_kb + half
    b_gb = b_vb + per

    is_q = (j < b_f) | ((j >= b_qb) & (j < b_kb))
    is_f = (j >= b_f) & (j < b_i)
    is_gate = ((j >= b_g) & (j < b_qb)) | (j >= b_gb)
    is_plain = jnp.logical_not(is_q | is_f | is_gate)

    @pl.when(is_q)
    def _():
        for rows in chunks:
            u_ref[rows, :] = (project(rows) * scale).astype(BF16)

    @pl.when(is_plain)
    def _():
        for rows in chunks:
            u_ref[rows, :] = project(rows).astype(BF16)

    @pl.when(is_gate)
    def _():
        for rows in chunks:
            acc = project(rows)
            u_ref[rows, :] = (acc * _sigmoid(acc)).astype(BF16)

    @pl.when(is_f)
    def _():
        lb = lb_ref[...]
        log_lb = jnp.log(lb)
        log_1m_lb = jnp.log1p(-lb)
        for rows in chunks:
            acc = project(rows)
            e = jnp.exp(-jnp.abs(acc))
            r = 1.0 / (1.0 + e)
            sig_neg = jnp.where(acc >= 0, e * r, r)
            ls = jnp.minimum(acc, 0.0) - jnp.log1p(e)
            b = log_1m_lb + ls
            m = jnp.maximum(log_lb, b)
            logf_ref[rows, :] = m + jnp.log1p(jnp.exp(-jnp.abs(log_lb - b)))
            u_ref[rows, :] = ((1.0 - lb) * sig_neg).astype(BF16)


def _inproj_main(x, nw, w_in, layer, lb, *, tm, tn):
    t = x.shape[0]
    per = 1024 // tn
    nj = MAIN_COLS // tn
    kern = functools.partial(_inproj_main_kernel, tn=tn)
    return pl.pallas_call(
        kern,
        grid=(t // tm, nj),
        in_specs=[
            _single_buffered((tm, D_MODEL), lambda i, j: (i, 0)),
            pl.BlockSpec((1, D_MODEL), lambda i, j: (0, 0)),
            pl.BlockSpec((None, tn, D_MODEL), lambda i, j: (layer, j, 0)),
            pl.BlockSpec((1, tn), lambda i, j: (0, jnp.clip(j - per, 0, per - 1))),
        ],
        out_specs=[
            pl.BlockSpec((tm, tn), lambda i, j: (i, j)),
            pl.BlockSpec((tm, tn), lambda i, j: (i, jnp.clip(j - per, 0, per - 1))),
        ],
        out_shape=[
            jax.ShapeDtypeStruct((t, MAIN_COLS), BF16),
            jax.ShapeDtypeStruct((t, HGRN_KEY), F32),
        ],
        scratch_shapes=[pltpu.VMEM((tm, D_MODEL), BF16)],
        compiler_params=_cparams(("parallel", "arbitrary")),
        name="inproj_main",
    )(x, nw, w_in, lb)


def _inproj_tail_kernel(x_ref, nw_ref, wm_ref, wgk_ref, up_ref, upb_ref, gates_ref, loggk_ref, h_scr):
    j = pl.program_id(1)
    chunks = _row_chunks(x_ref.shape[0])

    @pl.when(j == 0)
    def _():
        wgk = wgk_ref[0].astype(BF16)
        up = up_ref[...].astype(BF16)
        for rows in chunks:
            h = _rms(x_ref[rows, :], nw_ref[...]).astype(BF16)
            h_scr[rows, :] = h
            code = _bdot_nt(h, wgk)
            lg = _bdot(code.astype(BF16), up) + upb_ref[...]
            loggk_ref[rows, :] = _log_sigmoid(lg) * (1.0 / GLA_GATE_NORMALIZER)

    for rows in chunks:
        acc = _bdot_nt(h_scr[rows, :], wm_ref[0].astype(BF16))
        gates_ref[rows, :] = _sigmoid(acc).astype(BF16)


def _inproj_tail(x, nw, w_in, layer, up_w, up_b, *, tm, tn):
    t = x.shape[0]
    gate_row0 = MAIN_COLS + GLA_GATE_RANK
    return pl.pallas_call(
        _inproj_tail_kernel,
        grid=(t // tm, GATE_COLS // tn),
        in_specs=[
            _single_buffered((tm, D_MODEL), lambda i, j: (i, 0)),
            pl.BlockSpec((1, D_MODEL), lambda i, j: (0, 0)),
            pl.BlockSpec((pl.Element(1), pl.Element(tn), pl.Element(D_MODEL)),
                         lambda i, j: (layer, pl.multiple_of(gate_row0 + j * tn, 8), 0)),
            pl.BlockSpec((pl.Element(1), pl.Element(GLA_GATE_RANK), pl.Element(D_MODEL)),
                         lambda i, j: (layer, MAIN_COLS, 0)),
            pl.BlockSpec((GLA_GATE_RANK, GLA_KEY), lambda i, j: (0, 0)),
            pl.BlockSpec((1, GLA_KEY), lambda i, j: (0, 0)),
        ],
        out_specs=[
            pl.BlockSpec((tm, tn), lambda i, j: (i, j)),
            pl.BlockSpec((tm, GLA_KEY), lambda i, j: (i, 0)),
        ],
        out_shape=[
            jax.ShapeDtypeStruct((t, GATE_COLS), BF16),
            jax.ShapeDtypeStruct((t, GLA_KEY), F32),
        ],
        scratch_shapes=[pltpu.VMEM((tm, D_MODEL), BF16)],
        compiler_params=_cparams(("parallel", "arbitrary")),
        name="inproj_tail",
    )(x, nw, w_in, w_in, up_w, up_b)


def _recurrence_constants(c):
    levels = int(np.log2(c))
    t = np.arange(c)[:, None]
    s = np.arange(c)[None, :]
    mats = [(s <= t), (s > t)]
    masks = [np.eye(c, dtype=bool)]
    for l in range(levels):
        b = 1 << l
        tb = (t // (2 * b)) * 2 * b + b - 1
        is_i = (t // b) % 2 == 1
        mats.append(np.where(is_i, (s > tb) & (s <= t), (s > t) & (s <= tb)))
        i = np.arange(c)[:, None]
        j = np.arange(c)[None, :]
        masks.append((i // (2 * b) == j // (2 * b)) & ((i // b) % 2 == 1) & ((j // b) % 2 == 0))
    m = np.concatenate(mats, axis=0).astype(np.float32)
    m2 = np.concatenate([m, m], axis=1)
    masks2 = np.tile(np.stack(masks).astype(np.float32), (1, 1, 2))
    return jnp.asarray(m2, BF16), jnp.asarray(masks2)


def _recurrence_kernel(q_ref, k_ref, g_ref, v_ref, sg_ref, nw_ref, m2_ref, masks_ref, y_ref, st_ref,
                       *, c, dk, dv, nchunk, npairs):
    levels = int(np.log2(c))
    w = 2 * dk

    @pl.when(pl.program_id(1) == 0)
    def _():
        st_ref[...] = jnp.zeros_like(st_ref)

    row = lax.broadcasted_iota(jnp.int32, (c, w), 0)
    shift = int(np.log2(dk))
    same_head = ((lax.broadcasted_iota(jnp.int32, (w, w), 0) >> shift)
                 == (lax.broadcasted_iota(jnp.int32, (w, w), 1) >> shift))
    nw = nw_ref[...]

    def block_diag(b):
        return jnp.where(same_head, jnp.concatenate([b, b], axis=0), jnp.zeros((), b.dtype))

    def pair_nt(a, b):
        return lax.dot_general(a, block_diag(b), _NT, preferred_element_type=F32)

    def chunk_body(ci, carry):
        r0 = pl.multiple_of(ci * c, c)
        rows = pl.ds(r0, c)
        for pp in range(npairs):
            pair_chunk(rows, pp)
        return carry

    def pair_chunk(rows, pp):
        kcols = slice(pp * w, (pp + 1) * w)
        vbase = pp * 2 * dv
        g = g_ref[rows, kcols]
        g_hi = g.astype(BF16)
        g_lo = (g - g_hi.astype(F32)).astype(BF16)
        d_all = _bdot(m2_ref[...], jnp.concatenate([g_hi, g_lo], axis=0))
        e_all = jnp.exp(d_all)
        q = q_ref[rows, kcols]
        k = k_ref[rows, kcols]
        qf = q.astype(F32)
        kf = k.astype(F32)
        e_cum = e_all[0:c]
        q_in = (qf * e_cum).astype(BF16)
        k_out = (kf * e_all[c:2 * c]).astype(BF16)
        scores = pair_nt(q, k) * masks_ref[0]
        for l in range(levels):
            b = 1 << l
            if b >= 8:
                parts = [(qf if (r // b) % 2 == 1 else kf)[r:r + b] for r in range(0, c, b)]
                qk = jnp.concatenate(parts, axis=0)
            else:
                qk = jnp.where(((row >> l) & 1) == 1, qf, kf)
            x = (qk * e_all[(l + 2) * c:(l + 3) * c]).astype(BF16)
            scores = scores + pair_nt(x, x) * masks_ref[l + 1]
        sb = scores.astype(BF16)
        v = v_ref[rows, vbase:vbase + 2 * dv]
        decay = e_cum[c - 1:c, :]
        if dv == dk:
            st = st_ref[pp]
            o = _bdot(sb, block_diag(v)) + lax.dot_general(
                q_in, st.astype(BF16), _NT, preferred_element_type=F32)
            upd = lax.dot_general(v, k_out, _TN, preferred_element_type=F32)
            st_ref[pp] = st * decay + jnp.where(same_head, upd, 0.0)
            outs = [o[:, :dv], o[:, dv:]]
        else:
            outs = []
            for hh in range(2):
                ks = slice(hh * dk, (hh + 1) * dk)
                vh = v[:, hh * dv:(hh + 1) * dv]
                st = st_ref[2 * pp + hh]
                outs.append(_bdot(sb[:, ks], vh) + lax.dot_general(
                    q_in[:, ks], st.astype(BF16), _NT, preferred_element_type=F32))
                st_ref[2 * pp + hh] = st * decay[:, ks] + lax.dot_general(
                    vh, k_out[:, ks], _TN, preferred_element_type=F32)
        for hh, o in enumerate(outs):
            vs = slice(vbase + hh * dv, vbase + (hh + 1) * dv)
            o = o * lax.rsqrt(jnp.mean(o * o, axis=-1, keepdims=True) + EPS) * nw
            y_ref[rows, vs] = (o * sg_ref[rows, vs].astype(F32)).astype(BF16)

    lax.fori_loop(0, nchunk, chunk_body, 0)


def _recurrence(q_arr, q_off, k_arr, k_off, g_arr, v_arr, v_off, sg_arr, sg_off, nw, consts,
                *, heads, dv, hb, tb, c):
    t = q_arr.shape[0]
    dk = HEAD_K
    assert hb % 2 == 0 and c == dk
    npairs = hb // 2
    m2, masks = consts
    kern = functools.partial(_recurrence_kernel, c=c, dk=dk, dv=dv, nchunk=tb // c, npairs=npairs)
    wk, wv = hb * dk, hb * dv
    st_shape = (npairs, 2 * dv, 2 * dk) if dv == dk else (hb, dv, dk)
    return pl.pallas_call(
        kern,
        grid=(heads // hb, t // tb),
        in_specs=[
            pl.BlockSpec((tb, wk), lambda h, i: (i, q_off + h)),
            pl.BlockSpec((tb, wk), lambda h, i: (i, k_off + h)),
            pl.BlockSpec((tb, wk), lambda h, i: (i, h)),
            pl.BlockSpec((tb, wv), lambda h, i: (i, v_off + h)),
            pl.BlockSpec((tb, wv), lambda h, i: (i, sg_off + h)),
            pl.BlockSpec((1, dv), lambda h, i: (0, 0)),
            pl.BlockSpec(m2.shape, lambda h, i: (0, 0)),
            pl.BlockSpec(masks.shape, lambda h, i: (0, 0, 0)),
        ],
        out_specs=pl.BlockSpec((tb, wv), lambda h, i: (i, h)),
        out_shape=jax.ShapeDtypeStruct((t, heads * dv), BF16),
        scratch_shapes=[pltpu.VMEM(st_shape, F32)],
        compiler_params=_cparams(("parallel", "arbitrary")),
        name="recurrence_dv%d" % dv,
    )(q_arr, k_arr, g_arr, v_arr, sg_arr, nw, m2, masks)


def _merge_kernel(ya_ref, yb_ref, wa_ref, wb_ref, ga_ref, gb_ref, o_ref):
    pa = _bdot(ya_ref[...], wa_ref[...].astype(BF16))
    pb = _bdot(yb_ref[...], wb_ref[...].astype(BF16))
    o_ref[...] = (ga_ref[...].astype(F32) * pa + gb_ref[...].astype(F32) * pb).astype(BF16)


def _merge(ya, yb, wa, wb, layer, gates, *, tm, tn):
    t = ya.shape[0]
    nj = D_MODEL // tn
    return pl.pallas_call(
        _merge_kernel,
        grid=(t // tm, nj),
        in_specs=[
            pl.BlockSpec((tm, HGRN_VAL), lambda i, j: (i, 0)),
            pl.BlockSpec((tm, GLA_VAL), lambda i, j: (i, 0)),
            pl.BlockSpec((None, HGRN_VAL, tn), lambda i, j: (layer, 0, j)),
            pl.BlockSpec((None, GLA_VAL, tn), lambda i, j: (layer, 0, j)),
            pl.BlockSpec((tm, tn), lambda i, j: (i, j)),
            pl.BlockSpec((tm, tn), lambda i, j: (i, nj + j)),
        ],
        out_specs=pl.BlockSpec((tm, tn), lambda i, j: (i, j)),
        out_shape=jax.ShapeDtypeStruct((t, D_MODEL), BF16),
        compiler_params=_cparams(("parallel", "arbitrary")),
        name="merge",
    )(ya, yb, wa, wb, gates, gates)


def _outproj_kernel(m_ref, w_ref, x_ref, o_ref):
    o_ref[...] = x_ref[...] + _bdot(m_ref[...], w_ref[...].astype(BF16))


def _outproj(merged, w_out, layer, x, *, tm, tn):
    t = x.shape[0]
    return pl.pallas_call(
        _outproj_kernel,
        grid=(t // tm, D_MODEL // tn),
        in_specs=[
            pl.BlockSpec((tm, D_MODEL), lambda i, j: (i, 0)),
            pl.BlockSpec((None, D_MODEL, tn), lambda i, j: (layer, 0, j)),
            pl.BlockSpec((tm, tn), lambda i, j: (i, j)),
        ],
        out_specs=pl.BlockSpec((tm, tn), lambda i, j: (i, j)),
        out_shape=jax.ShapeDtypeStruct((t, D_MODEL), F32),
        compiler_params=_cparams(("parallel", "arbitrary")),
        name="outproj",
    )(merged, w_out, x)


def _dense_ffn_kernel(x_ref, nw_ref, wg_ref, wu_ref, wd_ref, o_ref, h_scr):
    j = pl.program_id(1)

    @pl.when(j == 0)
    def _():
        x = x_ref[...]
        h_scr[...] = _rms(x, nw_ref[...]).astype(BF16)
        o_ref[...] = x

    h = h_scr[...]
    g = _bdot(h, wg_ref[...].astype(BF16))
    u = _bdot(h, wu_ref[...].astype(BF16))
    a = (g * _sigmoid(g) * u).astype(BF16)
    o_ref[...] += _bdot(a, wd_ref[...].astype(BF16))


def _dense_ffn(x, nw, wg, wu, wd, idx, *, tm, tf):
    t = x.shape[0]
    return pl.pallas_call(
        _dense_ffn_kernel,
        grid=(t // tm, D_FF_DENSE // tf),
        in_specs=[
            pl.BlockSpec((tm, D_MODEL), lambda i, j: (i, 0)),
            pl.BlockSpec((1, D_MODEL), lambda i, j: (0, 0)),
            pl.BlockSpec((None, D_MODEL, tf), lambda i, j: (idx, 0, j)),
            pl.BlockSpec((None, D_MODEL, tf), lambda i, j: (idx, 0, j)),
            pl.BlockSpec((None, tf, D_MODEL), lambda i, j: (idx, j, 0)),
        ],
        out_specs=pl.BlockSpec((tm, D_MODEL), lambda i, j: (i, 0)),
        out_shape=jax.ShapeDtypeStruct((t, D_MODEL), F32),
        scratch_shapes=[pltpu.VMEM((tm, D_MODEL), BF16)],
        compiler_params=_cparams(("parallel", "arbitrary")),
        name="dense_ffn",
    )(x, nw, wg, wu, wd)


def _router_kernel(x_ref, nw_ref, rw_ref, rb_ref, h_ref, idx_ref, wts_ref):
    h = _rms(x_ref[...], nw_ref[...])
    h_ref[...] = h
    logits = jnp.dot(h, rw_ref[...], preferred_element_type=F32,
                     precision=lax.Precision.HIGHEST) + rb_ref[...]
    lane_i = lax.broadcasted_iota(jnp.int32, logits.shape, 1)
    lane = lane_i.astype(F32)
    m1 = jnp.max(logits, axis=-1, keepdims=True)
    i1 = jnp.min(jnp.where(logits == m1, lane, float(LANE)), axis=-1, keepdims=True)
    rest = jnp.where(lane == i1, -jnp.inf, logits)
    m2 = jnp.max(rest, axis=-1, keepdims=True)
    i2 = jnp.min(jnp.where(rest == m2, lane, float(LANE)), axis=-1, keepdims=True)
    e = jnp.exp(m2 - m1)
    w1 = 1.0 / (1.0 + e)
    w2 = e * w1
    idx_ref[...] = jnp.where(lane_i == 0, i1, jnp.where(lane_i == 1, i2, 0.0)).astype(jnp.int32)
    wts_ref[...] = jnp.where(lane_i == 0, w1, jnp.where(lane_i == 1, w2, 0.0))


def _router(x, nw, rw_pad, rb_pad, *, tm):
    t = x.shape[0]
    return pl.pallas_call(
        _router_kernel,
        grid=(t // tm,),
        in_specs=[
            pl.BlockSpec((tm, D_MODEL), lambda i: (i, 0)),
            pl.BlockSpec((1, D_MODEL), lambda i: (0, 0)),
            pl.BlockSpec((D_MODEL, LANE), lambda i: (0, 0)),
            pl.BlockSpec((1, LANE), lambda i: (0, 0)),
        ],
        out_specs=[
            pl.BlockSpec((tm, D_MODEL), lambda i: (i, 0)),
            pl.BlockSpec((tm, LANE), lambda i: (i, 0)),
            pl.BlockSpec((tm, LANE), lambda i: (i, 0)),
        ],
        out_shape=[
            jax.ShapeDtypeStruct((t, D_MODEL), F32),
            jax.ShapeDtypeStruct((t, LANE), jnp.int32),
            jax.ShapeDtypeStruct((t, LANE), F32),
        ],
        compiler_params=_cparams(("parallel",)),
        name="router",
    )(x, nw, rw_pad, rb_pad)


def _dispatch_kernel(dest_ref, pad_start_ref, pad_count_ref, h_ref, xs_hbm, zero_scr, sem, *, nb, n_tok):
    step = pl.program_id(0)
    n_tok_steps = n_tok // nb

    def row_copy(src_ref, src_row, dst_row):
        return pltpu.make_async_copy(src_ref.at[pl.ds(src_row, 1), :],
                                     xs_hbm.at[pl.ds(dst_row, 1), :], sem.at[0])

    @pl.when(step < n_tok_steps)
    def _():
        base = step * (nb * TOP_K)

        def issue(r, c):
            for kk in range(TOP_K):
                row_copy(h_ref, r, dest_ref[base + r * TOP_K + kk]).start()
            return c

        lax.fori_loop(0, nb, issue, 0)

        def drain(r, c):
            for kk in range(TOP_K):
                row_copy(h_ref, 0, 0).wait()
            return c

        lax.fori_loop(0, nb, drain, 0)

    @pl.when(step >= n_tok_steps)
    def _():
        e = step - n_tok_steps
        zero_scr[...] = jnp.zeros_like(zero_scr)
        start = pad_start_ref[e]
        n = pad_count_ref[e]

        def issue(r, c):
            row_copy(zero_scr, 0, start + r).start()
            return c

        lax.fori_loop(0, n, issue, 0)

        def drain(r, c):
            row_copy(zero_scr, 0, 0).wait()
            return c

        lax.fori_loop(0, n, drain, 0)


def _dispatch(dest, pad_start, pad_count, h, t_pad, *, nb):
    n_tok = h.shape[0]
    n_tok_steps = n_tok // nb
    kern = functools.partial(_dispatch_kernel, nb=nb, n_tok=n_tok)
    grid_spec = pltpu.PrefetchScalarGridSpec(
        num_scalar_prefetch=3,
        grid=(n_tok_steps + N_EXPERTS,),
        in_specs=[pl.BlockSpec((nb, D_MODEL),
                               lambda i, d, ps, pc: (jnp.minimum(i, n_tok_steps - 1), 0))],
        out_specs=pl.BlockSpec(memory_space=pl.ANY),
        scratch_shapes=[pltpu.VMEM((8, D_MODEL), F32), pltpu.SemaphoreType.DMA((1,))],
    )
    return pl.pallas_call(
        kern,
        grid_spec=grid_spec,
        out_shape=jax.ShapeDtypeStruct((t_pad, D_MODEL), F32),
        compiler_params=_cparams(("arbitrary",)),
        name="dispatch",
    )(dest, pad_start, pad_count, h)


def _moe_ffn_kernel(te_ref, tv_ref, tb_ref, xs_hbm, wg_ref, wu_ref, wd_ref, ys_hbm,
                    x_scr, acc_scr, wg_scr, wu_scr, wd_scr, a_scr, pend_ref, sem_in, sem_out, *, sub, tm):
    i = pl.program_id(0)
    j = pl.program_id(1)
    last_i = pl.num_programs(0) - 1
    last_j = pl.num_programs(1) - 1
    valid = tv_ref[i]
    nsub = (valid + sub - 1) // sub
    tile_row0 = tb_ref[i] * tm

    def in_copy(s):
        r = pl.multiple_of(s * sub, sub)
        return pltpu.make_async_copy(xs_hbm.at[pl.ds(tile_row0 + r, sub), :],
                                     x_scr.at[pl.ds(r, sub), :], sem_in.at[0])

    def out_copy(s):
        r = pl.multiple_of(s * sub, sub)
        return pltpu.make_async_copy(acc_scr.at[pl.ds(r, sub), :],
                                     ys_hbm.at[pl.ds(tile_row0 + r, sub), :], sem_out.at[0])

    def wait_pending_out():
        def body(s, c):
            out_copy(0).wait()
            return c

        lax.fori_loop(0, pend_ref[0], body, 0)
        pend_ref[0] = 0

    @pl.when((i == 0) & (j == 0))
    def _():
        pend_ref[0] = 0

    @pl.when(valid > 0)
    def _():
        @pl.when(j == 0)
        def _():
            wait_pending_out()

            def start_in(s, c):
                in_copy(s).start()
                return c

            lax.fori_loop(0, nsub, start_in, 0)

            def zero_rows(s, c):
                acc_scr[pl.ds(pl.multiple_of(s * sub, sub), sub), :] = jnp.zeros((sub, D_MODEL), F32)
                return c

            lax.fori_loop(0, nsub, zero_rows, 0)

            def wait_in(s, c):
                in_copy(0).wait()
                return c

            lax.fori_loop(0, nsub, wait_in, 0)

        def cast_weights():
            wg_scr[...] = wg_ref[...].astype(BF16)
            wu_scr[...] = wu_ref[...].astype(BF16)
            wd_scr[...] = wd_ref[...].astype(BF16)

        def up(row0):
            h = x_scr[pl.ds(row0, sub), :].astype(BF16)
            g = _bdot(h, wg_scr[...])
            u = _bdot(h, wu_scr[...])
            return (g * _sigmoid(g) * u).astype(BF16)

        def down(row0, a):
            acc_scr[pl.ds(row0, sub), :] += _bdot(a, wd_scr[...])

        npair = nsub // 2
        blk = 2 * sub

        @pl.when(npair == 0)
        def _():
            cast_weights()

        @pl.when(npair > 0)
        def _():
            cast_weights()
            a_scr[0] = up(0)
            a_scr[1] = up(sub)

            def pair_body(p, c):
                base = pl.multiple_of(p * blk, blk)
                prev0 = a_scr[0]
                prev1 = a_scr[1]
                next0 = up(base)
                next1 = up(base + sub)
                down(base - blk, prev0)
                down(base - sub, prev1)
                a_scr[0] = next0
                a_scr[1] = next1
                return c

            lax.fori_loop(1, npair, pair_body, 0)
            tail = pl.multiple_of((npair - 1) * blk, blk)
            down(tail, a_scr[0])
            down(tail + sub, a_scr[1])

        @pl.when(nsub % 2 == 1)
        def _():
            last = pl.multiple_of((nsub - 1) * sub, sub)
            down(last, up(last))

        @pl.when(j == last_j)
        def _():
            def start_out(s, c):
                out_copy(s).start()
                return c

            lax.fori_loop(0, nsub, start_out, 0)
            pend_ref[0] = nsub

    @pl.when((i == last_i) & (j == last_j))
    def _():
        wait_pending_out()


def _moe_ffn(tile_expert, tile_valid, tile_blk, xs, wg, wu, wd, idx, *, tm, tf, sub):
    t_pad = xs.shape[0]
    n_tiles = t_pad // tm
    nf = D_FF_EXPERT // tf
    kern = functools.partial(_moe_ffn_kernel, sub=sub, tm=tm)

    def jj(i, j, tv):
        return jnp.where(tv[i] > 0, j, nf - 1)

    grid_spec = pltpu.PrefetchScalarGridSpec(
        num_scalar_prefetch=3,
        grid=(n_tiles, nf),
        in_specs=[
            pl.BlockSpec(memory_space=pl.ANY),
            pl.BlockSpec((None, None, D_MODEL, tf), lambda i, j, te, tv, tb: (idx, te[i], 0, jj(i, j, tv))),
            pl.BlockSpec((None, None, D_MODEL, tf), lambda i, j, te, tv, tb: (idx, te[i], 0, jj(i, j, tv))),
            pl.BlockSpec((None, None, tf, D_MODEL), lambda i, j, te, tv, tb: (idx, te[i], jj(i, j, tv), 0)),
        ],
        out_specs=pl.BlockSpec(memory_space=pl.ANY),
        scratch_shapes=[
            pltpu.VMEM((tm, D_MODEL), F32),
            pltpu.VMEM((tm, D_MODEL), F32),
            pltpu.VMEM((D_MODEL, tf), BF16),
            pltpu.VMEM((D_MODEL, tf), BF16),
            pltpu.VMEM((tf, D_MODEL), BF16),
            pltpu.VMEM((2, sub, tf), BF16),
            pltpu.SMEM((1,), jnp.int32),
            pltpu.SemaphoreType.DMA((1,)),
            pltpu.SemaphoreType.DMA((1,)),
        ],
    )
    return pl.pallas_call(
        kern,
        grid_spec=grid_spec,
        out_shape=jax.ShapeDtypeStruct((t_pad, D_MODEL), F32),
        compiler_params=_cparams(("arbitrary", "arbitrary")),
        name="moe_ffn",
    )(tile_expert, tile_valid, tile_blk, xs, wg, wu, wd)


def _combine_kernel(dest_ref, x_ref, wts_ref, fw_ref, ys_hbm, o_ref, ya_scr, yb_scr, sem,
                    *, nb, final_norm):
    step = pl.program_id(0)
    nsteps = pl.num_programs(0)
    slot = step % 2

    def row_copy(src_row, dst_ref, slot_, r):
        return pltpu.make_async_copy(ys_hbm.at[pl.ds(src_row, 1), :],
                                     dst_ref.at[slot_, pl.ds(r, 1), :], sem.at[slot_])

    def gather(block, slot_):
        def issue(r, c):
            s = (block * nb + r) * TOP_K
            row_copy(dest_ref[s], ya_scr, slot_, r).start()
            row_copy(dest_ref[s + 1], yb_scr, slot_, r).start()
            return c

        lax.fori_loop(0, nb, issue, 0)

    @pl.when(step == 0)
    def _():
        gather(0, 0)

    @pl.when(step + 1 < nsteps)
    def _():
        gather(step + 1, 1 - slot)

    def drain(r, c):
        row_copy(0, ya_scr, slot, 0).wait()
        row_copy(0, yb_scr, slot, 0).wait()
        return c

    lax.fori_loop(0, nb, drain, 0)

    w = wts_ref[...]
    out = x_ref[...] + w[:, 0:1] * ya_scr[slot] + w[:, 1:2] * yb_scr[slot]
    if final_norm:
        out = _rms(out, fw_ref[...])
    o_ref[...] = out


def _combine(dest, x, wts, fw, ys, *, nb, final_norm):
    t = x.shape[0]
    kern = functools.partial(_combine_kernel, nb=nb, final_norm=final_norm)
    grid_spec = pltpu.PrefetchScalarGridSpec(
        num_scalar_prefetch=1,
        grid=(t // nb,),
        in_specs=[
            pl.BlockSpec((nb, D_MODEL), lambda i, d: (i, 0)),
            pl.BlockSpec((nb, LANE), lambda i, d: (i, 0)),
            pl.BlockSpec((1, D_MODEL), lambda i, d: (0, 0)),
            pl.BlockSpec(memory_space=pl.ANY),
        ],
        out_specs=pl.BlockSpec((nb, D_MODEL), lambda i, d: (i, 0)),
        scratch_shapes=[
            pltpu.VMEM((2, nb, D_MODEL), F32),
            pltpu.VMEM((2, nb, D_MODEL), F32),
            pltpu.SemaphoreType.DMA((2,)),
        ],
    )
    return pl.pallas_call(
        kern,
        grid_spec=grid_spec,
        out_shape=jax.ShapeDtypeStruct((t, D_MODEL), F32),
        compiler_params=_cparams(("arbitrary",)),
        name="combine",
    )(dest, x, wts, fw, ys)


def _routing_tables(idx2, tm, sub):
    n_slots = idx2.shape[0] * TOP_K
    e_flat = idx2.reshape(-1)
    onehot = (e_flat[:, None] == jnp.arange(N_EXPERTS, dtype=jnp.int32)[None, :]).astype(jnp.int32)
    csum = jnp.cumsum(onehot, axis=0)
    counts = csum[-1]
    rank = jnp.sum(onehot * csum, axis=1) - 1
    padded = ((counts + tm - 1) // tm) * tm
    gend = jnp.cumsum(padded)
    gstart = gend - padded
    dest = (jnp.sum(onehot * gstart[None, :], axis=1) + rank).astype(jnp.int32)

    n_tiles = n_slots // tm + N_EXPERTS
    t_pad = n_tiles * tm
    tile_row0 = jnp.arange(n_tiles, dtype=jnp.int32) * tm
    n_used = gend[-1] // tm
    tile_blk = jnp.minimum(jnp.arange(n_tiles, dtype=jnp.int32), n_used - 1)
    blk_row0 = tile_blk * tm
    tile_expert = jnp.sum((blk_row0[:, None] >= gend[None, :]).astype(jnp.int32), axis=1)
    tile_expert = jnp.minimum(tile_expert, N_EXPERTS - 1)
    valid = jnp.clip(gstart[tile_expert] + counts[tile_expert] - tile_row0, 0, tm)
    tile_valid = jnp.where(jnp.arange(n_tiles) < n_used, valid, 0).astype(jnp.int32)

    pad_start = (gstart + counts).astype(jnp.int32)
    pad_count = (((counts + sub - 1) // sub) * sub - counts).astype(jnp.int32)
    return (dest, pad_start, pad_count, tile_expert.astype(jnp.int32), tile_valid,
            tile_blk.astype(jnp.int32), t_pad)


def _moe_block(x, nw, rw, rb, wg, wu, wd, idx, fw, final_norm):
    t = x.shape[0]
    tm = min(TM, t)
    rw_pad = jnp.pad(rw, ((0, 0), (0, LANE - N_EXPERTS)))
    rb_pad = jnp.pad(rb, (0, LANE - N_EXPERTS), constant_values=-1e30).reshape(1, LANE)
    moe_tm = min(MOE_TM, TOP_K * t)
    h, idx_l, wts = _router(x, nw, rw_pad, rb_pad, tm=tm)
    dest, pad_start, pad_count, tile_expert, tile_valid, tile_blk, t_pad = _routing_tables(
        idx_l[:, :TOP_K], moe_tm, MOE_SUB)
    xs = _dispatch(dest, pad_start, pad_count, h, t_pad, nb=ROW_BATCH)
    ys = _moe_ffn(tile_expert, tile_valid, tile_blk, xs, wg, wu, wd, idx, tm=moe_tm, tf=TF_MOE,
                  sub=MOE_SUB)
    return _combine(dest, x, wts, fw, ys, nb=ROW_BATCH, final_norm=final_norm)


def _final_norm_kernel(x_ref, w_ref, o_ref):
    o_ref[...] = _rms(x_ref[...], w_ref[...])


def _final_norm(x, w, *, tm):
    t = x.shape[0]
    return pl.pallas_call(
        _final_norm_kernel,
        grid=(t // tm,),
        in_specs=[pl.BlockSpec((tm, D_MODEL), lambda i: (i, 0)),
                  pl.BlockSpec((1, D_MODEL), lambda i: (0, 0))],
        out_specs=pl.BlockSpec((tm, D_MODEL), lambda i: (i, 0)),
        out_shape=jax.ShapeDtypeStruct((t, D_MODEL), F32),
        compiler_params=_cparams(("parallel",)),
        name="final_norm",
    )(x, w)


def _mixer(x, layer, mix_norm_w, w_in, lower_bound_logits, gk_up_w, gk_up_b, hgrn_norm_w, gla_norm_w,
           w_proj_a, w_proj_b, w_out, consts):
    t = x.shape[0]
    tm = min(TM_PROJ, t)
    tb = min(ATT_TB, t)
    nw = mix_norm_w[layer].reshape(1, D_MODEL)
    bounds = jnp.cumsum(jax.nn.softmax(lower_bound_logits.astype(F32), axis=0), axis=0)
    lb = (bounds[layer] - bounds[0]).reshape(1, HGRN_KEY)
    u, logf = _inproj_main(x, nw, w_in, layer, lb, tm=tm, tn=TN_IN)
    gates, loggk = _inproj_tail(x, nw, w_in, layer, gk_up_w[layer], gk_up_b[layer].reshape(1, GLA_KEY),
                                tm=tm, tn=TN_IN)
    hb_a, hb_b = 4, 4
    wa = hb_a * HEAD_K
    y_a = _recurrence(u, 0, u, 1024 // wa, logf, u, 2048 // wa, u, 3072 // wa,
                      hgrn_norm_w[layer].reshape(1, -1), consts,
                      heads=HGRN_HEADS, dv=128, hb=hb_a, tb=tb, c=ATT_CHUNK)
    wkb, wvb = hb_b * HEAD_K, hb_b * 256
    y_b = _recurrence(u, 4096 // wkb, u, 4608 // wkb, loggk, u, 5120 // wvb, u, 6144 // wvb,
                      gla_norm_w[layer].reshape(1, -1), consts,
                      heads=GLA_HEADS, dv=256, hb=hb_b, tb=tb, c=ATT_CHUNK)
    merged = _merge(y_a, y_b, w_proj_a, w_proj_b, layer, gates, tm=tm, tn=TN_OUT)
    return _outproj(merged, w_out, layer, x, tm=tm, tn=TN_OUT)


def kernel(x, mix_norm_w, w_in, lower_bound_logits, gk_up_w, gk_up_b, hgrn_norm_w, gla_norm_w, w_proj_a,
           w_proj_b, w_out, ffn_norm_w, dense_w_gate, dense_w_up, dense_w_down, router_w, router_b,
           moe_w_gate, moe_w_up, moe_w_down, final_norm_w):
    b, t, d = x.shape
    xs = x.reshape(b * t, d)
    consts = _recurrence_constants(ATT_CHUNK)
    fw = final_norm_w.reshape(1, D_MODEL)
    fused_final = False
    w_in_t = jnp.swapaxes(w_in, 1, 2)
    for layer in range(DEPTH):
        xs = _mixer(xs, layer, mix_norm_w, w_in_t, lower_bound_logits, gk_up_w, gk_up_b, hgrn_norm_w,
                    gla_norm_w, w_proj_a, w_proj_b, w_out, consts)
        nw = ffn_norm_w[layer].reshape(1, D_MODEL)
        j = layer // 2
        if layer % 2 == 0:
            xs = _dense_ffn(xs, nw, dense_w_gate, dense_w_up, dense_w_down, j,
                            tm=min(TM, xs.shape[0]), tf=TF_DENSE)
        else:
            fused_final = layer == DEPTH - 1
            xs = _moe_block(xs, nw, router_w[j], router_b[j], moe_w_gate, moe_w_up, moe_w_down, j,
                            fw, fused_final)
    if not fused_final:
        xs = _final_norm(xs, fw, tm=min(TM, xs.shape[0]))
    return xs.reshape(b, t, d)
```

```python
import functools

import numpy as np
import jax
import jax.numpy as jnp
from jax import lax
from jax.experimental import pallas as pl
from jax.experimental.pallas import tpu as pltpu

F32 = jnp.float32
BF16 = jnp.bfloat16

D_MODEL = 2048
DEPTH = 2
HGRN_HEADS = 8
HGRN_KEY = 1024
HGRN_VAL = 1024
GLA_HEADS = 4
GLA_KEY = 512
GLA_VAL = 1024
HEAD_K = 128
GLA_GATE_RANK = 16
GLA_GATE_NORMALIZER = 16.0
D_FF_DENSE = 5632
N_EXPERTS = 8
TOP_K = 2
D_FF_EXPERT = 7168
EPS = 1e-6

MAIN_COLS = 4 * 1024 + 2 * 512 + 2 * 1024
GATE_COLS = 2 * D_MODEL

LANE = 128
VMEM_LIMIT = 56 * 1024 * 1024

TM = 1024
TN_IN = 512
TN_OUT = 512
TF_DENSE = 256
TF_MOE = 256
MOE_TM = 2304
MOE_SUB = 256
ATT_CHUNK = 128
ATT_TB = 1024
ROW_BATCH = 256
ROW_CHUNKS = 4
TM_PROJ = 2048


def _cparams(sem):
    return pltpu.CompilerParams(dimension_semantics=sem, vmem_limit_bytes=VMEM_LIMIT)


def _rms(x, w):
    return x * lax.rsqrt(jnp.mean(x * x, axis=-1, keepdims=True) + EPS) * w


def _sigmoid(x):
    return 0.5 * jnp.tanh(0.5 * x) + 0.5


def _log_sigmoid(x):
    return jnp.minimum(x, 0.0) - jnp.log1p(jnp.exp(-jnp.abs(x)))


def _bdot(a, b):
    return jnp.dot(a, b, preferred_element_type=F32)


_NT = (((1,), (1,)), ((), ()))
_TN = (((0,), (0,)), ((), ()))


def _bdot_nt(a, b):
    return lax.dot_general(a, b, _NT, preferred_element_type=F32)


def _row_chunks(tm):
    step = max(tm // ROW_CHUNKS, 8)
    return [slice(r, r + step) for r in range(0, tm, step)]


def _single_buffered(block_shape, index_map):
    return pl.BlockSpec(block_shape, index_map, pipeline_mode=pl.Buffered(1))


def _inproj_main_kernel(x_ref, nw_ref, w_ref, lb_ref, u_ref, logf_ref, h_scr, *, tn):
    j = pl.program_id(1)
    chunks = _row_chunks(x_ref.shape[0])

    @pl.when(j == 0)
    def _():
        for rows in chunks:
            h_scr[rows, :] = _rms(x_ref[rows, :], nw_ref[...]).astype(BF16)

    def project(rows):
        return _bdot_nt(h_scr[rows, :], w_ref[...].astype(BF16))

    per = 1024 // tn
    half = 512 // tn
    scale = HEAD_K ** -0.5
    b_f, b_i, b_g = per, 2 * per, 3 * per
    b_qb = 4 * per
    b_kb = b_qb + half
    b_vb = b_kb + half
    b_gb = b_vb + per

    is_q = (j < b_f) | ((j >= b_qb) & (j < b_kb))
    is_f = (j >= b_f) & (j < b_i)
    is_gate = ((j >= b_g) & (j < b_qb)) | (j >= b_gb)
    is_plain = jnp.logical_not(is_q | is_f | is_gate)

    @pl.when(is_q)
    def _():
        for rows in chunks:
            u_ref[rows, :] = (project(rows) * scale).astype(BF16)

    @pl.when(is_plain)
    def _():
        for rows in chunks:
            u_ref[rows, :] = project(rows).astype(BF16)

    @pl.when(is_gate)
    def _():
        for rows in chunks:
            acc = project(rows)
            u_ref[rows, :] = (acc * _sigmoid(acc)).astype(BF16)

    @pl.when(is_f)
    def _():
        lb = lb_ref[...]
        log_lb = jnp.log(lb)
        log_1m_lb = jnp.log1p(-lb)
        for rows in chunks:
            acc = project(rows)
            e = jnp.exp(-jnp.abs(acc))
            r = 1.0 / (1.0 + e)
            sig_neg = jnp.where(acc >= 0, e * r, r)
            ls = jnp.minimum(acc, 0.0) - jnp.log1p(e)
            b = log_1m_lb + ls
            m = jnp.maximum(log_lb, b)
            logf_ref[rows, :] = m + jnp.log1p(jnp.exp(-jnp.abs(log_lb - b)))
            u_ref[rows, :] = ((1.0 - lb) * sig_neg).astype(BF16)


def _inproj_main(x, nw, w_in, layer, lb, *, tm, tn):
    t = x.shape[0]
    per = 1024 // tn
    nj = MAIN_COLS // tn
    kern = functools.partial(_inproj_main_kernel, tn=tn)
    return pl.pallas_call(
        kern,
        grid=(t // tm, nj),
        in_specs=[
            _single_buffered((tm, D_MODEL), lambda i, j: (i, 0)),
            pl.BlockSpec((1, D_MODEL), lambda i, j: (0, 0)),
            pl.BlockSpec((None, tn, D_MODEL), lambda i, j: (layer, j, 0)),
            pl.BlockSpec((1, tn), lambda i, j: (0, jnp.clip(j - per, 0, per - 1))),
        ],
        out_specs=[
            pl.BlockSpec((tm, tn), lambda i, j: (i, j)),
            pl.BlockSpec((tm, tn), lambda i, j: (i, jnp.clip(j - per, 0, per - 1))),
        ],
        out_shape=[
            jax.ShapeDtypeStruct((t, MAIN_COLS), BF16),
            jax.ShapeDtypeStruct((t, HGRN_KEY), F32),
        ],
        scratch_shapes=[pltpu.VMEM((tm, D_MODEL), BF16)],
        compiler_params=_cparams(("parallel", "arbitrary")),
        name="inproj_main",
    )(x, nw, w_in, lb)


def _inproj_tail_kernel(x_ref, nw_ref, wm_ref, wgk_ref, up_ref, upb_ref, gates_ref, loggk_ref, h_scr):
    j = pl.program_id(1)
    chunks = _row_chunks(x_ref.shape[0])

    @pl.when(j == 0)
    def _():
        wgk = wgk_ref[0].astype(BF16)
        up = up_ref[...].astype(BF16)
        for rows in chunks:
            h = _rms(x_ref[rows, :], nw_ref[...]).astype(BF16)
            h_scr[rows, :] = h
            code = _bdot_nt(h, wgk)
            lg = _bdot(code.astype(BF16), up) + upb_ref[...]
            loggk_ref[rows, :] = _log_sigmoid(lg) * (1.0 / GLA_GATE_NORMALIZER)

    for rows in chunks:
        acc = _bdot_nt(h_scr[rows, :], wm_ref[0].astype(BF16))
        gates_ref[rows, :] = _sigmoid(acc).astype(BF16)


def _inproj_tail(x, nw, w_in, layer, up_w, up_b, *, tm, tn):
    t = x.shape[0]
    gate_row0 = MAIN_COLS + GLA_GATE_RANK
    return pl.pallas_call(
        _inproj_tail_kernel,
        grid=(t // tm, GATE_COLS // tn),
        in_specs=[
            _single_buffered((tm, D_MODEL), lambda i, j: (i, 0)),
            pl.BlockSpec((1, D_MODEL), lambda i, j: (0, 0)),
            pl.BlockSpec((pl.Element(1), pl.Element(tn), pl.Element(D_MODEL)),
                         lambda i, j: (layer, pl.multiple_of(gate_row0 + j * tn, 8), 0)),
            pl.BlockSpec((pl.Element(1), pl.Element(GLA_GATE_RANK), pl.Element(D_MODEL)),
                         lambda i, j: (layer, MAIN_COLS, 0)),
            pl.BlockSpec((GLA_GATE_RANK, GLA_KEY), lambda i, j: (0, 0)),
            pl.BlockSpec((1, GLA_KEY), lambda i, j: (0, 0)),
        ],
        out_specs=[
            pl.BlockSpec((tm, tn), lambda i, j: (i, j)),
            pl.BlockSpec((tm, GLA_KEY), lambda i, j: (i, 0)),
        ],
        out_shape=[
            jax.ShapeDtypeStruct((t, GATE_COLS), BF16),
            jax.ShapeDtypeStruct((t, GLA_KEY), F32),
        ],
        scratch_shapes=[pltpu.VMEM((tm, D_MODEL), BF16)],
        compiler_params=_cparams(("parallel", "arbitrary")),
        name="inproj_tail",
    )(x, nw, w_in, w_in, up_w, up_b)


def _recurrence_constants(c):
    levels = int(np.log2(c))
    t = np.arange(c)[:, None]
    s = np.arange(c)[None, :]
    mats = [(s <= t), (s > t)]
    masks = [np.eye(c, dtype=bool)]
    for l in range(levels):
        b = 1 << l
        tb = (t // (2 * b)) * 2 * b + b - 1
        is_i = (t // b) % 2 == 1
        mats.append(np.where(is_i, (s > tb) & (s <= t), (s > t) & (s <= tb)))
        i = np.arange(c)[:, None]
        j = np.arange(c)[None, :]
        masks.append((i // (2 * b) == j // (2 * b)) & ((i // b) % 2 == 1) & ((j // b) % 2 == 0))
    m = np.concatenate(mats, axis=0).astype(np.float32)
    m2 = np.concatenate([m, m], axis=1)
    masks2 = np.tile(np.stack(masks).astype(np.float32), (1, 1, 2))
    return jnp.asarray(m2, BF16), jnp.asarray(masks2)


def _recurrence_kernel(q_ref, k_ref, g_ref, v_ref, sg_ref, nw_ref, m2_ref, masks_ref, y_ref, st_ref,
                       *, c, dk, dv, nchunk, npairs):
    levels = int(np.log2(c))
    w = 2 * dk

    @pl.when(pl.program_id(1) == 0)
    def _():
        st_ref[...] = jnp.zeros_like(st_ref)

    row = lax.broadcasted_iota(jnp.int32, (c, w), 0)
    shift = int(np.log2(dk))
    same_head = ((lax.broadcasted_iota(jnp.int32, (w, w), 0) >> shift)
                 == (lax.broadcasted_iota(jnp.int32, (w, w), 1) >> shift))
    nw = nw_ref[...]

    def block_diag(b):
        return jnp.where(same_head, jnp.concatenate([b, b], axis=0), jnp.zeros((), b.dtype))

    def pair_nt(a, b):
        return lax.dot_general(a, block_diag(b), _NT, preferred_element_type=F32)

    def chunk_body(ci, carry):
        r0 = pl.multiple_of(ci * c, c)
        rows = pl.ds(r0, c)
        for pp in range(npairs):
            pair_chunk(rows, pp)
        return carry

    def pair_chunk(rows, pp):
        kcols = slice(pp * w, (pp + 1) * w)
        vbase = pp * 2 * dv
        g = g_ref[rows, kcols]
        g_hi = g.astype(BF16)
        g_lo = (g - g_hi.astype(F32)).astype(BF16)
        d_all = _bdot(m2_ref[...], jnp.concatenate([g_hi, g_lo], axis=0))
        e_all = jnp.exp(d_all)
        q = q_ref[rows, kcols]
        k = k_ref[rows, kcols]
        qf = q.astype(F32)
        kf = k.astype(F32)
        e_cum = e_all[0:c]
        q_in = (qf * e_cum).astype(BF16)
        k_out = (kf * e_all[c:2 * c]).astype(BF16)
        scores = pair_nt(q, k) * masks_ref[0]
        for l in range(levels):
            b = 1 << l
            if b >= 8:
                parts = [(qf if (r // b) % 2 == 1 else kf)[r:r + b] for r in range(0, c, b)]
                qk = jnp.concatenate(parts, axis=0)
            else:
                qk = jnp.where(((row >> l) & 1) == 1, qf, kf)
            x = (qk * e_all[(l + 2) * c:(l + 3) * c]).astype(BF16)
            scores = scores + pair_nt(x, x) * masks_ref[l + 1]
        sb = scores.astype(BF16)
        v = v_ref[rows, vbase:vbase + 2 * dv]
        decay = e_cum[c - 1:c, :]
        if dv == dk:
            st = st_ref[pp]
            o = _bdot(sb, block_diag(v)) + lax.dot_general(
                q_in, st.astype(BF16), _NT, preferred_element_type=F32)
            upd = lax.dot_general(v, k_out, _TN, preferred_element_type=F32)
            st_ref[pp] = st * decay + jnp.where(same_head, upd, 0.0)
            outs = [o[:, :dv], o[:, dv:]]
        else:
            outs = []
            for hh in range(2):
                ks = slice(hh * dk, (hh + 1) * dk)
                vh = v[:, hh * dv:(hh + 1) * dv]
                st = st_ref[2 * pp + hh]
                outs.append(_bdot(sb[:, ks], vh) + lax.dot_general(
                    q_in[:, ks], st.astype(BF16), _NT, preferred_element_type=F32))
                st_ref[2 * pp + hh] = st * decay[:, ks] + lax.dot_general(
                    vh, k_out[:, ks], _TN, preferred_element_type=F32)
        for hh, o in enumerate(outs):
            vs = slice(vbase + hh * dv, vbase + (hh + 1) * dv)
            o = o * lax.rsqrt(jnp.mean(o * o, axis=-1, keepdims=True) + EPS) * nw
            y_ref[rows, vs] = (o * sg_ref[rows, vs].astype(F32)).astype(BF16)

    lax.fori_loop(0, nchunk, chunk_body, 0)


def _recurrence(q_arr, q_off, k_arr, k_off, g_arr, v_arr, v_off, sg_arr, sg_off, nw, consts,
                *, heads, dv, hb, tb, c):
    t = q_arr.shape[0]
    dk = HEAD_K
    assert hb % 2 == 0 and c == dk
    npairs = hb // 2
    m2, masks = consts
    kern = functools.partial(_recurrence_kernel, c=c, dk=dk, dv=dv, nchunk=tb // c, npairs=npairs)
    wk, wv = hb * dk, hb * dv
    st_shape = (npairs, 2 * dv, 2 * dk) if dv == dk else (hb, dv, dk)
    return pl.pallas_call(
        kern,
        grid=(heads // hb, t // tb),
        in_specs=[
            pl.BlockSpec((tb, wk), lambda h, i: (i, q_off + h)),
            pl.BlockSpec((tb, wk), lambda h, i: (i, k_off + h)),
            pl.BlockSpec((tb, wk), lambda h, i: (i, h)),
            pl.BlockSpec((tb, wv), lambda h, i: (i, v_off + h)),
            pl.BlockSpec((tb, wv), lambda h, i: (i, sg_off + h)),
            pl.BlockSpec((1, dv), lambda h, i: (0, 0)),
            pl.BlockSpec(m2.shape, lambda h, i: (0, 0)),
            pl.BlockSpec(masks.shape, lambda h, i: (0, 0, 0)),
        ],
        out_specs=pl.BlockSpec((tb, wv), lambda h, i: (i, h)),
        out_shape=jax.ShapeDtypeStruct((t, heads * dv), BF16),
        scratch_shapes=[pltpu.VMEM(st_shape, F32)],
        compiler_params=_cparams(("parallel", "arbitrary")),
        name="recurrence_dv%d" % dv,
    )(q_arr, k_arr, g_arr, v_arr, sg_arr, nw, m2, masks)


def _merge_kernel(ya_ref, yb_ref, wa_ref, wb_ref, ga_ref, gb_ref, o_ref):
    pa = _bdot(ya_ref[...], wa_ref[...].astype(BF16))
    pb = _bdot(yb_ref[...], wb_ref[...].astype(BF16))
    o_ref[...] = (ga_ref[...].astype(F32) * pa + gb_ref[...].astype(F32) * pb).astype(BF16)


def _merge(ya, yb, wa, wb, layer, gates, *, tm, tn):
    t = ya.shape[0]
    nj = D_MODEL // tn
    return pl.pallas_call(
        _merge_kernel,
        grid=(t // tm, nj),
        in_specs=[
            pl.BlockSpec((tm, HGRN_VAL), lambda i, j: (i, 0)),
            pl.BlockSpec((tm, GLA_VAL), lambda i, j: (i, 0)),
            pl.BlockSpec((None, HGRN_VAL, tn), lambda i, j: (layer, 0, j)),
            pl.BlockSpec((None, GLA_VAL, tn), lambda i, j: (layer, 0, j)),
            pl.BlockSpec((tm, tn), lambda i, j: (i, j)),
            pl.BlockSpec((tm, tn), lambda i, j: (i, nj + j)),
        ],
        out_specs=pl.BlockSpec((tm, tn), lambda i, j: (i, j)),
        out_shape=jax.ShapeDtypeStruct((t, D_MODEL), BF16),
        compiler_params=_cparams(("parallel", "arbitrary")),
        name="merge",
    )(ya, yb, wa, wb, gates, gates)


def _outproj_kernel(m_ref, w_ref, x_ref, o_ref):
    o_ref[...] = x_ref[...] + _bdot(m_ref[...], w_ref[...].astype(BF16))


def _outproj(merged, w_out, layer, x, *, tm, tn):
    t = x.shape[0]
    return pl.pallas_call(
        _outproj_kernel,
        grid=(t // tm, D_MODEL // tn),
        in_specs=[
            pl.BlockSpec((tm, D_MODEL), lambda i, j: (i, 0)),
            pl.BlockSpec((None, D_MODEL, tn), lambda i, j: (layer, 0, j)),
            pl.BlockSpec((tm, tn), lambda i, j: (i, j)),
        ],
        out_specs=pl.BlockSpec((tm, tn), lambda i, j: (i, j)),
        out_shape=jax.ShapeDtypeStruct((t, D_MODEL), F32),
        compiler_params=_cparams(("parallel", "arbitrary")),
        name="outproj",
    )(merged, w_out, x)


def _dense_ffn_kernel(x_ref, nw_ref, wg_ref, wu_ref, wd_ref, o_ref, h_scr):
    j = pl.program_id(1)

    @pl.when(j == 0)
    def _():
        x = x_ref[...]
        h_scr[...] = _rms(x, nw_ref[...]).astype(BF16)
        o_ref[...] = x

    h = h_scr[...]
    g = _bdot(h, wg_ref[...].astype(BF16))
    u = _bdot(h, wu_ref[...].astype(BF16))
    a = (g * _sigmoid(g) * u).astype(BF16)
    o_ref[...] += _bdot(a, wd_ref[...].astype(BF16))


def _dense_ffn(x, nw, wg, wu, wd, idx, *, tm, tf):
    t = x.shape[0]
    return pl.pallas_call(
        _dense_ffn_kernel,
        grid=(t // tm, D_FF_DENSE // tf),
        in_specs=[
            pl.BlockSpec((tm, D_MODEL), lambda i, j: (i, 0)),
            pl.BlockSpec((1, D_MODEL), lambda i, j: (0, 0)),
            pl.BlockSpec((None, D_MODEL, tf), lambda i, j: (idx, 0, j)),
            pl.BlockSpec((None, D_MODEL, tf), lambda i, j: (idx, 0, j)),
            pl.BlockSpec((None, tf, D_MODEL), lambda i, j: (idx, j, 0)),
        ],
        out_specs=pl.BlockSpec((tm, D_MODEL), lambda i, j: (i, 0)),
        out_shape=jax.ShapeDtypeStruct((t, D_MODEL), F32),
        scratch_shapes=[pltpu.VMEM((tm, D_MODEL), BF16)],
        compiler_params=_cparams(("parallel", "arbitrary")),
        name="dense_ffn",
    )(x, nw, wg, wu, wd)


def _router_kernel(x_ref, nw_ref, rw_ref, rb_ref, h_ref, idx_ref, wts_ref):
    h = _rms(x_ref[...], nw_ref[...])
    h_ref[...] = h
    logits = jnp.dot(h, rw_ref[...], preferred_element_type=F32,
                     precision=lax.Precision.HIGHEST) + rb_ref[...]
    lane_i = lax.broadcasted_iota(jnp.int32, logits.shape, 1)
    lane = lane_i.astype(F32)
    m1 = jnp.max(logits, axis=-1, keepdims=True)
    i1 = jnp.min(jnp.where(logits == m1, lane, float(LANE)), axis=-1, keepdims=True)
    rest = jnp.where(lane == i1, -jnp.inf, logits)
    m2 = jnp.max(rest, axis=-1, keepdims=True)
    i2 = jnp.min(jnp.where(rest == m2, lane, float(LANE)), axis=-1, keepdims=True)
    e = jnp.exp(m2 - m1)
    w1 = 1.0 / (1.0 + e)
    w2 = e * w1
    idx_ref[...] = jnp.where(lane_i == 0, i1, jnp.where(lane_i == 1, i2, 0.0)).astype(jnp.int32)
    wts_ref[...] = jnp.where(lane_i == 0, w1, jnp.where(lane_i == 1, w2, 0.0))


def _router(x, nw, rw_pad, rb_pad, *, tm):
    t = x.shape[0]
    return pl.pallas_call(
        _router_kernel,
        grid=(t // tm,),
        in_specs=[
            pl.BlockSpec((tm, D_MODEL), lambda i: (i, 0)),
            pl.BlockSpec((1, D_MODEL), lambda i: (0, 0)),
            pl.BlockSpec((D_MODEL, LANE), lambda i: (0, 0)),
            pl.BlockSpec((1, LANE), lambda i: (0, 0)),
        ],
        out_specs=[
            pl.BlockSpec((tm, D_MODEL), lambda i: (i, 0)),
            pl.BlockSpec((tm, LANE), lambda i: (i, 0)),
            pl.BlockSpec((tm, LANE), lambda i: (i, 0)),
        ],
        out_shape=[
            jax.ShapeDtypeStruct((t, D_MODEL), F32),
            jax.ShapeDtypeStruct((t, LANE), jnp.int32),
            jax.ShapeDtypeStruct((t, LANE), F32),
        ],
        compiler_params=_cparams(("parallel",)),
        name="router",
    )(x, nw, rw_pad, rb_pad)


def _dispatch_kernel(dest_ref, pad_start_ref, pad_count_ref, h_ref, xs_hbm, zero_scr, sem, *, nb, n_tok):
    step = pl.program_id(0)
    n_tok_steps = n_tok // nb

    def row_copy(src_ref, src_row, dst_row):
        return pltpu.make_async_copy(src_ref.at[pl.ds(src_row, 1), :],
                                     xs_hbm.at[pl.ds(dst_row, 1), :], sem.at[0])

    @pl.when(step < n_tok_steps)
    def _():
        base = step * (nb * TOP_K)

        def issue(r, c):
            for kk in range(TOP_K):
                row_copy(h_ref, r, dest_ref[base + r * TOP_K + kk]).start()
            return c

        lax.fori_loop(0, nb, issue, 0, unroll=4)

        for kk in range(TOP_K):
            pltpu.make_async_copy(h_ref, xs_hbm.at[pl.ds(0, nb), :], sem.at[0]).wait()

    @pl.when(step >= n_tok_steps)
    def _():
        e = step - n_tok_steps
        zero_scr[...] = jnp.zeros_like(zero_scr)
        start = pad_start_ref[e]
        n = pad_count_ref[e]

        def issue(r, c):
            row_copy(zero_scr, 0, start + r).start()
            return c

        lax.fori_loop(0, n, issue, 0)

        def drain(r, c):
            row_copy(zero_scr, 0, 0).wait()
            return c

        lax.fori_loop(0, n, drain, 0)


def _dispatch(dest, pad_start, pad_count, h, t_pad, *, nb):
    n_tok = h.shape[0]
    n_tok_steps = n_tok // nb
    kern = functools.partial(_dispatch_kernel, nb=nb, n_tok=n_tok)
    grid_spec = pltpu.PrefetchScalarGridSpec(
        num_scalar_prefetch=3,
        grid=(n_tok_steps + N_EXPERTS,),
        in_specs=[pl.BlockSpec((nb, D_MODEL),
                               lambda i, d, ps, pc: (jnp.minimum(i, n_tok_steps - 1), 0))],
        out_specs=pl.BlockSpec(memory_space=pl.ANY),
        scratch_shapes=[pltpu.VMEM((8, D_MODEL), F32), pltpu.SemaphoreType.DMA((1,))],
    )
    return pl.pallas_call(
        kern,
        grid_spec=grid_spec,
        out_shape=jax.ShapeDtypeStruct((t_pad, D_MODEL), F32),
        compiler_params=_cparams(("arbitrary",)),
        name="dispatch",
    )(dest, pad_start, pad_count, h)


def _moe_ffn_kernel(te_ref, tv_ref, tb_ref, xs_hbm, wg_ref, wu_ref, wd_ref, ys_hbm,
                    x_scr, acc_scr, wg_scr, wu_scr, wd_scr, a_scr, pend_ref, sem_in, sem_out, *, sub, tm):
    i = pl.program_id(0)
    j = pl.program_id(1)
    last_i = pl.num_programs(0) - 1
    last_j = pl.num_programs(1) - 1
    valid = tv_ref[i]
    nsub = (valid + sub - 1) // sub
    tile_row0 = tb_ref[i] * tm

    def in_copy(s):
        r = pl.multiple_of(s * sub, sub)
        return pltpu.make_async_copy(xs_hbm.at[pl.ds(tile_row0 + r, sub), :],
                                     x_scr.at[pl.ds(r, sub), :], sem_in.at[0])

    def out_copy(s):
        r = pl.multiple_of(s * sub, sub)
        return pltpu.make_async_copy(acc_scr.at[pl.ds(r, sub), :],
                                     ys_hbm.at[pl.ds(tile_row0 + r, sub), :], sem_out.at[0])

    def wait_pending_out():
        def body(s, c):
            out_copy(0).wait()
            return c

        lax.fori_loop(0, pend_ref[0], body, 0)
        pend_ref[0] = 0

    @pl.when((i == 0) & (j == 0))
    def _():
        pend_ref[0] = 0

    @pl.when(valid > 0)
    def _():
        @pl.when(j == 0)
        def _():
            wait_pending_out()

            def start_in(s, c):
                in_copy(s).start()
                return c

            lax.fori_loop(0, nsub, start_in, 0)

            def zero_rows(s, c):
                acc_scr[pl.ds(pl.multiple_of(s * sub, sub), sub), :] = jnp.zeros((sub, D_MODEL), F32)
                return c

            lax.fori_loop(0, nsub, zero_rows, 0)

            def wait_in(s, c):
                in_copy(0).wait()
                return c

            lax.fori_loop(0, nsub, wait_in, 0)

        def cast_weights():
            wg_scr[...] = wg_ref[...].astype(BF16)
            wu_scr[...] = wu_ref[...].astype(BF16)
            wd_scr[...] = wd_ref[...].astype(BF16)

        def up(row0):
            h = x_scr[pl.ds(row0, sub), :].astype(BF16)
            g = _bdot(h, wg_scr[...])
            u = _bdot(h, wu_scr[...])
            return (g * _sigmoid(g) * u).astype(BF16)

        def down(row0, a):
            acc_scr[pl.ds(row0, sub), :] += _bdot(a, wd_scr[...])

        npair = nsub // 2
        blk = 2 * sub

        @pl.when(npair == 0)
        def _():
            cast_weights()

        @pl.when(npair > 0)
        def _():
            cast_weights()
            a_scr[0] = up(0)
            a_scr[1] = up(sub)

            def pair_body(p, c):
                base = pl.multiple_of(p * blk, blk)
                prev0 = a_scr[0]
                prev1 = a_scr[1]
                next0 = up(base)
                next1 = up(base + sub)
                down(base - blk, prev0)
                down(base - sub, prev1)
                a_scr[0] = next0
                a_scr[1] = next1
                return c

            lax.fori_loop(1, npair, pair_body, 0)
            tail = pl.multiple_of((npair - 1) * blk, blk)
            down(tail, a_scr[0])
            down(tail + sub, a_scr[1])

        @pl.when(nsub % 2 == 1)
        def _():
            last = pl.multiple_of((nsub - 1) * sub, sub)
            down(last, up(last))

        @pl.when(j == last_j)
        def _():
            def start_out(s, c):
                out_copy(s).start()
                return c

            lax.fori_loop(0, nsub, start_out, 0)
            pend_ref[0] = nsub

    @pl.when((i == last_i) & (j == last_j))
    def _():
        wait_pending_out()


def _moe_ffn(tile_expert, tile_valid, tile_blk, xs, wg, wu, wd, idx, *, tm, tf, sub):
    t_pad = xs.shape[0]
    n_tiles = t_pad // tm
    nf = D_FF_EXPERT // tf
    kern = functools.partial(_moe_ffn_kernel, sub=sub, tm=tm)

    def jj(i, j, tv):
        return jnp.where(tv[i] > 0, j, nf - 1)

    grid_spec = pltpu.PrefetchScalarGridSpec(
        num_scalar_prefetch=3,
        grid=(n_tiles, nf),
        in_specs=[
            pl.BlockSpec(memory_space=pl.ANY),
            pl.BlockSpec((None, None, D_MODEL, tf), lambda i, j, te, tv, tb: (idx, te[i], 0, jj(i, j, tv))),
            pl.BlockSpec((None, None, D_MODEL, tf), lambda i, j, te, tv, tb: (idx, te[i], 0, jj(i, j, tv))),
            pl.BlockSpec((None, None, tf, D_MODEL), lambda i, j, te, tv, tb: (idx, te[i], jj(i, j, tv), 0)),
        ],
        out_specs=pl.BlockSpec(memory_space=pl.ANY),
        scratch_shapes=[
            pltpu.VMEM((tm, D_MODEL), F32),
            pltpu.VMEM((tm, D_MODEL), F32),
            pltpu.VMEM((D_MODEL, tf), BF16),
            pltpu.VMEM((D_MODEL, tf), BF16),
            pltpu.VMEM((tf, D_MODEL), BF16),
            pltpu.VMEM((2, sub, tf), BF16),
            pltpu.SMEM((1,), jnp.int32),
            pltpu.SemaphoreType.DMA((1,)),
            pltpu.SemaphoreType.DMA((1,)),
        ],
    )
    return pl.pallas_call(
        kern,
        grid_spec=grid_spec,
        out_shape=jax.ShapeDtypeStruct((t_pad, D_MODEL), F32),
        compiler_params=_cparams(("arbitrary", "arbitrary")),
        name="moe_ffn",
    )(tile_expert, tile_valid, tile_blk, xs, wg, wu, wd)


def _combine_kernel(dest_ref, x_ref, wts_ref, fw_ref, ys_hbm, o_ref, ya_scr, yb_scr, sem,
                    *, nb, final_norm):
    step = pl.program_id(0)
    nsteps = pl.num_programs(0)
    slot = step % 2

    def row_copy(src_row, dst_ref, slot_, r):
        return pltpu.make_async_copy(ys_hbm.at[pl.ds(src_row, 1), :],
                                     dst_ref.at[slot_, pl.ds(r, 1), :], sem.at[slot_])

    def gather(block, slot_):
        def issue(r, c):
            s = (block * nb + r) * TOP_K
            row_copy(dest_ref[s], ya_scr, slot_, r).start()
            row_copy(dest_ref[s + 1], yb_scr, slot_, r).start()
            return c

        lax.fori_loop(0, nb, issue, 0, unroll=4)

    @pl.when(step == 0)
    def _():
        gather(0, 0)

    @pl.when(step + 1 < nsteps)
    def _():
        gather(step + 1, 1 - slot)

    for buf in (ya_scr, yb_scr):
        pltpu.make_async_copy(ys_hbm.at[pl.ds(0, nb), :], buf.at[slot], sem.at[slot]).wait()

    w = wts_ref[...]
    out = x_ref[...] + w[:, 0:1] * ya_scr[slot] + w[:, 1:2] * yb_scr[slot]
    if final_norm:
        out = _rms(out, fw_ref[...])
    o_ref[...] = out


def _combine(dest, x, wts, fw, ys, *, nb, final_norm):
    t = x.shape[0]
    kern = functools.partial(_combine_kernel, nb=nb, final_norm=final_norm)
    grid_spec = pltpu.PrefetchScalarGridSpec(
        num_scalar_prefetch=1,
        grid=(t // nb,),
        in_specs=[
            pl.BlockSpec((nb, D_MODEL), lambda i, d: (i, 0)),
            pl.BlockSpec((nb, LANE), lambda i, d: (i, 0)),
            pl.BlockSpec((1, D_MODEL), lambda i, d: (0, 0)),
            pl.BlockSpec(memory_space=pl.ANY),
        ],
        out_specs=pl.BlockSpec((nb, D_MODEL), lambda i, d: (i, 0)),
        scratch_shapes=[
            pltpu.VMEM((2, nb, D_MODEL), F32),
            pltpu.VMEM((2, nb, D_MODEL), F32),
            pltpu.SemaphoreType.DMA((2,)),
        ],
    )
    return pl.pallas_call(
        kern,
        grid_spec=grid_spec,
        out_shape=jax.ShapeDtypeStruct((t, D_MODEL), F32),
        compiler_params=_cparams(("arbitrary",)),
        name="combine",
    )(dest, x, wts, fw, ys)


def _routing_tables(idx2, tm, sub):
    n_slots = idx2.shape[0] * TOP_K
    e_flat = idx2.reshape(-1)
    onehot = (e_flat[:, None] == jnp.arange(N_EXPERTS, dtype=jnp.int32)[None, :]).astype(jnp.int32)
    csum = jnp.cumsum(onehot, axis=0)
    counts = csum[-1]
    rank = jnp.sum(onehot * csum, axis=1) - 1
    padded = ((counts + tm - 1) // tm) * tm
    gend = jnp.cumsum(padded)
    gstart = gend - padded
    dest = (jnp.sum(onehot * gstart[None, :], axis=1) + rank).astype(jnp.int32)

    n_tiles = n_slots // tm + N_EXPERTS
    t_pad = n_tiles * tm
    tile_row0 = jnp.arange(n_tiles, dtype=jnp.int32) * tm
    n_used = gend[-1] // tm
    tile_blk = jnp.minimum(jnp.arange(n_tiles, dtype=jnp.int32), n_used - 1)
    blk_row0 = tile_blk * tm
    tile_expert = jnp.sum((blk_row0[:, None] >= gend[None, :]).astype(jnp.int32), axis=1)
    tile_expert = jnp.minimum(tile_expert, N_EXPERTS - 1)
    valid = jnp.clip(gstart[tile_expert] + counts[tile_expert] - tile_row0, 0, tm)
    tile_valid = jnp.where(jnp.arange(n_tiles) < n_used, valid, 0).astype(jnp.int32)

    pad_start = (gstart + counts).astype(jnp.int32)
    pad_count = (((counts + sub - 1) // sub) * sub - counts).astype(jnp.int32)
    return (dest, pad_start, pad_count, tile_expert.astype(jnp.int32), tile_valid,
            tile_blk.astype(jnp.int32), t_pad)


def _moe_block(x, nw, rw, rb, wg, wu, wd, idx, fw, final_norm):
    t = x.shape[0]
    tm = min(TM, t)
    rw_pad = jnp.pad(rw, ((0, 0), (0, LANE - N_EXPERTS)))
    rb_pad = jnp.pad(rb, (0, LANE - N_EXPERTS), constant_values=-1e30).reshape(1, LANE)
    moe_tm = min(MOE_TM, TOP_K * t)
    h, idx_l, wts = _router(x, nw, rw_pad, rb_pad, tm=tm)
    dest, pad_start, pad_count, tile_expert, tile_valid, tile_blk, t_pad = _routing_tables(
        idx_l[:, :TOP_K], moe_tm, MOE_SUB)
    xs = _dispatch(dest, pad_start, pad_count, h, t_pad, nb=ROW_BATCH)
    ys = _moe_ffn(tile_expert, tile_valid, tile_blk, xs, wg, wu, wd, idx, tm=moe_tm, tf=TF_MOE,
                  sub=MOE_SUB)
    return _combine(dest, x, wts, fw, ys, nb=ROW_BATCH, final_norm=final_norm)


def _final_norm_kernel(x_ref, w_ref, o_ref):
    o_ref[...] = _rms(x_ref[...], w_ref[...])


def _final_norm(x, w, *, tm):
    t = x.shape[0]
    return pl.pallas_call(
        _final_norm_kernel,
        grid=(t // tm,),
        in_specs=[pl.BlockSpec((tm, D_MODEL), lambda i: (i, 0)),
                  pl.BlockSpec((1, D_MODEL), lambda i: (0, 0))],
        out_specs=pl.BlockSpec((tm, D_MODEL), lambda i: (i, 0)),
        out_shape=jax.ShapeDtypeStruct((t, D_MODEL), F32),
        compiler_params=_cparams(("parallel",)),
        name="final_norm",
    )(x, w)


def _mixer(x, layer, mix_norm_w, w_in, lower_bound_logits, gk_up_w, gk_up_b, hgrn_norm_w, gla_norm_w,
           w_proj_a, w_proj_b, w_out, consts):
    t = x.shape[0]
    tm = min(TM_PROJ, t)
    tb = min(ATT_TB, t)
    nw = mix_norm_w[layer].reshape(1, D_MODEL)
    bounds = jnp.cumsum(jax.nn.softmax(lower_bound_logits.astype(F32), axis=0), axis=0)
    lb = (bounds[layer] - bounds[0]).reshape(1, HGRN_KEY)
    u, logf = _inproj_main(x, nw, w_in, layer, lb, tm=tm, tn=TN_IN)
    gates, loggk = _inproj_tail(x, nw, w_in, layer, gk_up_w[layer], gk_up_b[layer].reshape(1, GLA_KEY),
                                tm=tm, tn=TN_IN)
    hb_a, hb_b = 4, 4
    wa = hb_a * HEAD_K
    y_a = _recurrence(u, 0, u, 1024 // wa, logf, u, 2048 // wa, u, 3072 // wa,
                      hgrn_norm_w[layer].reshape(1, -1), consts,
                      heads=HGRN_HEADS, dv=128, hb=hb_a, tb=tb, c=ATT_CHUNK)
    wkb, wvb = hb_b * HEAD_K, hb_b * 256
    y_b = _recurrence(u, 4096 // wkb, u, 4608 // wkb, loggk, u, 5120 // wvb, u, 6144 // wvb,
                      gla_norm_w[layer].reshape(1, -1), consts,
                      heads=GLA_HEADS, dv=256, hb=hb_b, tb=tb, c=ATT_CHUNK)
    merged = _merge(y_a, y_b, w_proj_a, w_proj_b, layer, gates, tm=tm, tn=TN_OUT)
    return _outproj(merged, w_out, layer, x, tm=tm, tn=TN_OUT)


def kernel(x, mix_norm_w, w_in, lower_bound_logits, gk_up_w, gk_up_b, hgrn_norm_w, gla_norm_w, w_proj_a,
           w_proj_b, w_out, ffn_norm_w, dense_w_gate, dense_w_up, dense_w_down, router_w, router_b,
           moe_w_gate, moe_w_up, moe_w_down, final_norm_w):
    b, t, d = x.shape
    xs = x.reshape(b * t, d)
    consts = _recurrence_constants(ATT_CHUNK)
    fw = final_norm_w.reshape(1, D_MODEL)
    fused_final = False
    w_in_t = jnp.swapaxes(w_in, 1, 2)
    for layer in range(DEPTH):
        xs = _mixer(xs, layer, mix_norm_w, w_in_t, lower_bound_logits, gk_up_w, gk_up_b, hgrn_norm_w,
                    gla_norm_w, w_proj_a, w_proj_b, w_out, consts)
        nw = ffn_norm_w[layer].reshape(1, D_MODEL)
        j = layer // 2
        if layer % 2 == 0:
            xs = _dense_ffn(xs, nw, dense_w_gate, dense_w_up, dense_w_down, j,
                            tm=min(TM, xs.shape[0]), tf=TF_DENSE)
        else:
            fused_final = layer == DEPTH - 1
            xs = _moe_block(xs, nw, router_w[j], router_b[j], moe_w_gate, moe_w_up, moe_w_down, j,
                            fw, fused_final)
    if not fused_final:
        xs = _final_norm(xs, fw, tm=min(TM, xs.shape[0]))
    return xs.reshape(b, t, d)
```

```python
import functools

import numpy as np
import jax
import jax.numpy as jnp
from jax import lax
from jax.experimental import pallas as pl
from jax.experimental.pallas import tpu as pltpu

F32 = jnp.float32
BF16 = jnp.bfloat16

D_MODEL = 2048
DEPTH = 2
HGRN_HEADS = 8
HGRN_KEY = 1024
HGRN_VAL = 1024
GLA_HEADS = 4
GLA_KEY = 512
GLA_VAL = 1024
HEAD_K = 128
GLA_GATE_RANK = 16
GLA_GATE_NORMALIZER = 16.0
D_FF_DENSE = 5632
N_EXPERTS = 8
TOP_K = 2
D_FF_EXPERT = 7168
EPS = 1e-6

MAIN_COLS = 4 * 1024 + 2 * 512 + 2 * 1024
GATE_COLS = 2 * D_MODEL

LANE = 128
VMEM_LIMIT = 56 * 1024 * 1024

TM = 1024
TN_IN = 512
TN_OUT = 512
TF_DENSE = 256
TF_MOE = 256
MOE_TM = 2304
MOE_SUB = 256
ATT_CHUNK = 128
ATT_TB = 1024
ROW_BATCH = 256
ROW_CHUNKS = 4
TM_PROJ = 2048


def _cparams(sem):
    return pltpu.CompilerParams(dimension_semantics=sem, vmem_limit_bytes=VMEM_LIMIT)


def _rms(x, w):
    return x * lax.rsqrt(jnp.mean(x * x, axis=-1, keepdims=True) + EPS) * w


def _sigmoid(x):
    return 0.5 * jnp.tanh(0.5 * x) + 0.5


def _log_sigmoid(x):
    return jnp.minimum(x, 0.0) - jnp.log1p(jnp.exp(-jnp.abs(x)))


def _bdot(a, b):
    return jnp.dot(a, b, preferred_element_type=F32)


_NT = (((1,), (1,)), ((), ()))
_TN = (((0,), (0,)), ((), ()))


def _bdot_nt(a, b):
    return lax.dot_general(a, b, _NT, preferred_element_type=F32)


def _row_chunks(tm):
    step = max(tm // ROW_CHUNKS, 8)
    return [slice(r, r + step) for r in range(0, tm, step)]


def _single_buffered(block_shape, index_map):
    return pl.BlockSpec(block_shape, index_map, pipeline_mode=pl.Buffered(1))


def _norm_cast_kernel(x_ref, w_ref, h_ref):
    h_ref[...] = _rms(x_ref[...], w_ref[...]).astype(BF16)


def _norm_cast(x, w, *, tm):
    t = x.shape[0]
    return pl.pallas_call(
        _norm_cast_kernel,
        grid=(t // tm,),
        in_specs=[pl.BlockSpec((tm, D_MODEL), lambda i: (i, 0)),
                  pl.BlockSpec((1, D_MODEL), lambda i: (0, 0))],
        out_specs=pl.BlockSpec((tm, D_MODEL), lambda i: (i, 0)),
        out_shape=jax.ShapeDtypeStruct((t, D_MODEL), BF16),
        compiler_params=_cparams(("parallel",)),
        name="norm_cast",
    )(x, w)


def _inproj_main_kernel(h_ref, w_ref, lb_ref, u_ref, logf_ref, *, tn):
    j = pl.program_id(1)
    chunks = _row_chunks(h_ref.shape[0])

    def project(rows):
        return _bdot_nt(h_ref[rows, :], w_ref[...].astype(BF16))

    per = 1024 // tn
    half = 512 // tn
    scale = HEAD_K ** -0.5
    b_f, b_i, b_g = per, 2 * per, 3 * per
    b_qb = 4 * per
    b_kb = b_qb + half
    b_vb = b_kb + half
    b_gb = b_vb + per

    is_q = (j < b_f) | ((j >= b_qb) & (j < b_kb))
    is_f = (j >= b_f) & (j < b_i)
    is_gate = ((j >= b_g) & (j < b_qb)) | (j >= b_gb)
    is_plain = jnp.logical_not(is_q | is_f | is_gate)

    @pl.when(is_q)
    def _():
        for rows in chunks:
            u_ref[rows, :] = (project(rows) * scale).astype(BF16)

    @pl.when(is_plain)
    def _():
        for rows in chunks:
            u_ref[rows, :] = project(rows).astype(BF16)

    @pl.when(is_gate)
    def _():
        for rows in chunks:
            acc = project(rows)
            u_ref[rows, :] = (acc * _sigmoid(acc)).astype(BF16)

    @pl.when(is_f)
    def _():
        lb = lb_ref[...]
        log_lb = jnp.log(lb)
        log_1m_lb = jnp.log1p(-lb)
        for rows in chunks:
            acc = project(rows)
            e = jnp.exp(-jnp.abs(acc))
            r = 1.0 / (1.0 + e)
            sig_neg = jnp.where(acc >= 0, e * r, r)
            ls = jnp.minimum(acc, 0.0) - jnp.log1p(e)
            b = log_1m_lb + ls
            m = jnp.maximum(log_lb, b)
            logf_ref[rows, :] = m + jnp.log1p(jnp.exp(-jnp.abs(log_lb - b)))
            u_ref[rows, :] = ((1.0 - lb) * sig_neg).astype(BF16)


def _inproj_main(h, w_in, layer, lb, *, tm, tn):
    t = h.shape[0]
    per = 1024 // tn
    nj = MAIN_COLS // tn
    kern = functools.partial(_inproj_main_kernel, tn=tn)
    return pl.pallas_call(
        kern,
        grid=(t // tm, nj),
        in_specs=[
            pl.BlockSpec((tm, D_MODEL), lambda i, j: (i, 0)),
            pl.BlockSpec((None, tn, D_MODEL), lambda i, j: (layer, j, 0)),
            pl.BlockSpec((1, tn), lambda i, j: (0, jnp.clip(j - per, 0, per - 1))),
        ],
        out_specs=[
            pl.BlockSpec((tm, tn), lambda i, j: (i, j)),
            pl.BlockSpec((tm, tn), lambda i, j: (i, jnp.clip(j - per, 0, per - 1))),
        ],
        out_shape=[
            jax.ShapeDtypeStruct((t, MAIN_COLS), BF16),
            jax.ShapeDtypeStruct((t, HGRN_KEY), F32),
        ],
        compiler_params=_cparams(("parallel", "arbitrary")),
        name="inproj_main",
    )(h, w_in, lb)


def _inproj_tail_kernel(h_ref, wm_ref, wgk_ref, up_ref, upb_ref, gates_ref, loggk_ref):
    j = pl.program_id(1)
    chunks = _row_chunks(h_ref.shape[0])

    @pl.when(j == 0)
    def _():
        wgk = wgk_ref[0].astype(BF16)
        up = up_ref[...].astype(BF16)
        for rows in chunks:
            code = _bdot_nt(h_ref[rows, :], wgk)
            lg = _bdot(code.astype(BF16), up) + upb_ref[...]
            loggk_ref[rows, :] = _log_sigmoid(lg) * (1.0 / GLA_GATE_NORMALIZER)

    for rows in chunks:
        acc = _bdot_nt(h_ref[rows, :], wm_ref[0].astype(BF16))
        gates_ref[rows, :] = _sigmoid(acc).astype(BF16)


def _inproj_tail(h, w_in, layer, up_w, up_b, *, tm, tn):
    t = h.shape[0]
    gate_row0 = MAIN_COLS + GLA_GATE_RANK
    return pl.pallas_call(
        _inproj_tail_kernel,
        grid=(t // tm, GATE_COLS // tn),
        in_specs=[
            pl.BlockSpec((tm, D_MODEL), lambda i, j: (i, 0)),
            pl.BlockSpec((pl.Element(1), pl.Element(tn), pl.Element(D_MODEL)),
                         lambda i, j: (layer, pl.multiple_of(gate_row0 + j * tn, 8), 0)),
            pl.BlockSpec((pl.Element(1), pl.Element(GLA_GATE_RANK), pl.Element(D_MODEL)),
                         lambda i, j: (layer, MAIN_COLS, 0)),
            pl.BlockSpec((GLA_GATE_RANK, GLA_KEY), lambda i, j: (0, 0)),
            pl.BlockSpec((1, GLA_KEY), lambda i, j: (0, 0)),
        ],
        out_specs=[
            pl.BlockSpec((tm, tn), lambda i, j: (i, j)),
            pl.BlockSpec((tm, GLA_KEY), lambda i, j: (i, 0)),
        ],
        out_shape=[
            jax.ShapeDtypeStruct((t, GATE_COLS), BF16),
            jax.ShapeDtypeStruct((t, GLA_KEY), F32),
        ],
        compiler_params=_cparams(("parallel", "arbitrary")),
        name="inproj_tail",
    )(h, w_in, w_in, up_w, up_b)


def _recurrence_constants(c):
    levels = int(np.log2(c))
    t = np.arange(c)[:, None]
    s = np.arange(c)[None, :]
    mats = [(s <= t), (s > t)]
    masks = [np.eye(c, dtype=bool)]
    for l in range(levels):
        b = 1 << l
        tb = (t // (2 * b)) * 2 * b + b - 1
        is_i = (t // b) % 2 == 1
        mats.append(np.where(is_i, (s > tb) & (s <= t), (s > t) & (s <= tb)))
        i = np.arange(c)[:, None]
        j = np.arange(c)[None, :]
        masks.append((i // (2 * b) == j // (2 * b)) & ((i // b) % 2 == 1) & ((j // b) % 2 == 0))
    m = np.concatenate(mats, axis=0).astype(np.float32)
    m2 = np.concatenate([m, m], axis=1)
    masks2 = np.tile(np.stack(masks).astype(np.float32), (1, 1, 2))
    return jnp.asarray(m2, BF16), jnp.asarray(masks2)


def _recurrence_kernel(q_ref, k_ref, g_ref, v_ref, sg_ref, nw_ref, m2_ref, masks_ref, y_ref, st_ref,
                       *, c, dk, dv, nchunk, npairs):
    levels = int(np.log2(c))
    w = 2 * dk

    @pl.when(pl.program_id(1) == 0)
    def _():
        st_ref[...] = jnp.zeros_like(st_ref)

    row = lax.broadcasted_iota(jnp.int32, (c, w), 0)
    shift = int(np.log2(dk))
    same_head = ((lax.broadcasted_iota(jnp.int32, (w, w), 0) >> shift)
                 == (lax.broadcasted_iota(jnp.int32, (w, w), 1) >> shift))
    nw = nw_ref[...]

    def block_diag(b):
        return jnp.where(same_head, jnp.concatenate([b, b], axis=0), jnp.zeros((), b.dtype))

    def pair_nt(a, b):
        return lax.dot_general(a, block_diag(b), _NT, preferred_element_type=F32)

    def chunk_body(ci, carry):
        r0 = pl.multiple_of(ci * c, c)
        rows = pl.ds(r0, c)
        for pp in range(npairs):
            pair_chunk(rows, pp)
        return carry

    def pair_chunk(rows, pp):
        kcols = slice(pp * w, (pp + 1) * w)
        vbase = pp * 2 * dv
        g = g_ref[rows, kcols]
        g_hi = g.astype(BF16)
        g_lo = (g - g_hi.astype(F32)).astype(BF16)
        d_all = _bdot(m2_ref[...], jnp.concatenate([g_hi, g_lo], axis=0))
        e_all = jnp.exp(d_all)
        q = q_ref[rows, kcols]
        k = k_ref[rows, kcols]
        qf = q.astype(F32)
        kf = k.astype(F32)
        e_cum = e_all[0:c]
        q_in = (qf * e_cum).astype(BF16)
        k_out = (kf * e_all[c:2 * c]).astype(BF16)
        scores = pair_nt(q, k) * masks_ref[0]
        for l in range(levels):
            b = 1 << l
            if b >= 8:
                parts = [(qf if (r // b) % 2 == 1 else kf)[r:r + b] for r in range(0, c, b)]
                qk = jnp.concatenate(parts, axis=0)
            else:
                qk = jnp.where(((row >> l) & 1) == 1, qf, kf)
            x = (qk * e_all[(l + 2) * c:(l + 3) * c]).astype(BF16)
            scores = scores + pair_nt(x, x) * masks_ref[l + 1]
        sb = scores.astype(BF16)
        v = v_ref[rows, vbase:vbase + 2 * dv]
        decay = e_cum[c - 1:c, :]
        if dv == dk:
            st = st_ref[pp]
            o = _bdot(sb, block_diag(v)) + lax.dot_general(
                q_in, st.astype(BF16), _NT, preferred_element_type=F32)
            upd = lax.dot_general(v, k_out, _TN, preferred_element_type=F32)
            st_ref[pp] = st * decay + jnp.where(same_head, upd, 0.0)
            outs = [o[:, :dv], o[:, dv:]]
        else:
            outs = []
            for hh in range(2):
                ks = slice(hh * dk, (hh + 1) * dk)
                vh = v[:, hh * dv:(hh + 1) * dv]
                st = st_ref[2 * pp + hh]
                outs.append(_bdot(sb[:, ks], vh) + lax.dot_general(
                    q_in[:, ks], st.astype(BF16), _NT, preferred_element_type=F32))
                st_ref[2 * pp + hh] = st * decay[:, ks] + lax.dot_general(
                    vh, k_out[:, ks], _TN, preferred_element_type=F32)
        for hh, o in enumerate(outs):
            vs = slice(vbase + hh * dv, vbase + (hh + 1) * dv)
            o = o * lax.rsqrt(jnp.mean(o * o, axis=-1, keepdims=True) + EPS) * nw
            y_ref[rows, vs] = (o * sg_ref[rows, vs].astype(F32)).astype(BF16)

    lax.fori_loop(0, nchunk, chunk_body, 0)


def _recurrence(q_arr, q_off, k_arr, k_off, g_arr, v_arr, v_off, sg_arr, sg_off, nw, consts,
                *, heads, dv, hb, tb, c):
    t = q_arr.shape[0]
    dk = HEAD_K
    assert hb % 2 == 0 and c == dk
    npairs = hb // 2
    m2, masks = consts
    kern = functools.partial(_recurrence_kernel, c=c, dk=dk, dv=dv, nchunk=tb // c, npairs=npairs)
    wk, wv = hb * dk, hb * dv
    st_shape = (npairs, 2 * dv, 2 * dk) if dv == dk else (hb, dv, dk)
    return pl.pallas_call(
        kern,
        grid=(heads // hb, t // tb),
        in_specs=[
            pl.BlockSpec((tb, wk), lambda h, i: (i, q_off + h)),
            pl.BlockSpec((tb, wk), lambda h, i: (i, k_off + h)),
            pl.BlockSpec((tb, wk), lambda h, i: (i, h)),
            pl.BlockSpec((tb, wv), lambda h, i: (i, v_off + h)),
            pl.BlockSpec((tb, wv), lambda h, i: (i, sg_off + h)),
            pl.BlockSpec((1, dv), lambda h, i: (0, 0)),
            pl.BlockSpec(m2.shape, lambda h, i: (0, 0)),
            pl.BlockSpec(masks.shape, lambda h, i: (0, 0, 0)),
        ],
        out_specs=pl.BlockSpec((tb, wv), lambda h, i: (i, h)),
        out_shape=jax.ShapeDtypeStruct((t, heads * dv), BF16),
        scratch_shapes=[pltpu.VMEM(st_shape, F32)],
        compiler_params=_cparams(("parallel", "arbitrary")),
        name="recurrence_dv%d" % dv,
    )(q_arr, k_arr, g_arr, v_arr, sg_arr, nw, m2, masks)


def _merge_kernel(ya_ref, yb_ref, wa_ref, wb_ref, ga_ref, gb_ref, o_ref):
    pa = _bdot(ya_ref[...], wa_ref[...].astype(BF16))
    pb = _bdot(yb_ref[...], wb_ref[...].astype(BF16))
    o_ref[...] = (ga_ref[...].astype(F32) * pa + gb_ref[...].astype(F32) * pb).astype(BF16)


def _merge(ya, yb, wa, wb, layer, gates, *, tm, tn):
    t = ya.shape[0]
    nj = D_MODEL // tn
    return pl.pallas_call(
        _merge_kernel,
        grid=(t // tm, nj),
        in_specs=[
            pl.BlockSpec((tm, HGRN_VAL), lambda i, j: (i, 0)),
            pl.BlockSpec((tm, GLA_VAL), lambda i, j: (i, 0)),
            pl.BlockSpec((None, HGRN_VAL, tn), lambda i, j: (layer, 0, j)),
            pl.BlockSpec((None, GLA_VAL, tn), lambda i, j: (layer, 0, j)),
            pl.BlockSpec((tm, tn), lambda i, j: (i, j)),
            pl.BlockSpec((tm, tn), lambda i, j: (i, nj + j)),
        ],
        out_specs=pl.BlockSpec((tm, tn), lambda i, j: (i, j)),
        out_shape=jax.ShapeDtypeStruct((t, D_MODEL), BF16),
        compiler_params=_cparams(("parallel", "arbitrary")),
        name="merge",
    )(ya, yb, wa, wb, gates, gates)


def _outproj_kernel(m_ref, w_ref, x_ref, o_ref):
    o_ref[...] = x_ref[...] + _bdot(m_ref[...], w_ref[...].astype(BF16))


def _outproj(merged, w_out, layer, x, *, tm, tn):
    t = x.shape[0]
    return pl.pallas_call(
        _outproj_kernel,
        grid=(t // tm, D_MODEL // tn),
        in_specs=[
            pl.BlockSpec((tm, D_MODEL), lambda i, j: (i, 0)),
            pl.BlockSpec((None, D_MODEL, tn), lambda i, j: (layer, 0, j)),
            pl.BlockSpec((tm, tn), lambda i, j: (i, j)),
        ],
        out_specs=pl.BlockSpec((tm, tn), lambda i, j: (i, j)),
        out_shape=jax.ShapeDtypeStruct((t, D_MODEL), F32),
        compiler_params=_cparams(("parallel", "arbitrary")),
        name="outproj",
    )(merged, w_out, x)


def _dense_ffn_kernel(x_ref, nw_ref, nw_next_ref, wg_ref, wu_ref, wd_ref, o_ref, hn_ref, h_scr):
    j = pl.program_id(1)
    chunks = _row_chunks(x_ref.shape[0])

    @pl.when(j == 0)
    def _():
        for rows in chunks:
            x = x_ref[rows, :]
            h_scr[rows, :] = _rms(x, nw_ref[...]).astype(BF16)
            o_ref[rows, :] = x

    wg = wg_ref[...].astype(BF16)
    wu = wu_ref[...].astype(BF16)
    wd = wd_ref[...].astype(BF16)
    for rows in chunks:
        h = h_scr[rows, :]
        g = _bdot(h, wg)
        u = _bdot(h, wu)
        a = (g * _sigmoid(g) * u).astype(BF16)
        o_ref[rows, :] += _bdot(a, wd)

    @pl.when(j == pl.num_programs(1) - 1)
    def _():
        for rows in chunks:
            hn_ref[rows, :] = _rms(o_ref[rows, :], nw_next_ref[...]).astype(BF16)


def _dense_ffn(x, nw, nw_next, wg, wu, wd, idx, *, tm, tf):
    t = x.shape[0]
    return pl.pallas_call(
        _dense_ffn_kernel,
        grid=(t // tm, D_FF_DENSE // tf),
        in_specs=[
            _single_buffered((tm, D_MODEL), lambda i, j: (i, 0)),
            pl.BlockSpec((1, D_MODEL), lambda i, j: (0, 0)),
            pl.BlockSpec((1, D_MODEL), lambda i, j: (0, 0)),
            pl.BlockSpec((None, D_MODEL, tf), lambda i, j: (idx, 0, j)),
            pl.BlockSpec((None, D_MODEL, tf), lambda i, j: (idx, 0, j)),
            pl.BlockSpec((None, tf, D_MODEL), lambda i, j: (idx, j, 0)),
        ],
        out_specs=[
            pl.BlockSpec((tm, D_MODEL), lambda i, j: (i, 0)),
            pl.BlockSpec((tm, D_MODEL), lambda i, j: (i, 0)),
        ],
        out_shape=[
            jax.ShapeDtypeStruct((t, D_MODEL), F32),
            jax.ShapeDtypeStruct((t, D_MODEL), BF16),
        ],
        scratch_shapes=[pltpu.VMEM((tm, D_MODEL), BF16)],
        compiler_params=_cparams(("parallel", "arbitrary")),
        name="dense_ffn",
    )(x, nw, nw_next, wg, wu, wd)


def _router_kernel(x_ref, nw_ref, rw_ref, rb_ref, h_ref, idx_ref, wts_ref):
    h = _rms(x_ref[...], nw_ref[...])
    h_ref[...] = h
    logits = jnp.dot(h, rw_ref[...], preferred_element_type=F32,
                     precision=lax.Precision.HIGHEST) + rb_ref[...]
    lane_i = lax.broadcasted_iota(jnp.int32, logits.shape, 1)
    lane = lane_i.astype(F32)
    m1 = jnp.max(logits, axis=-1, keepdims=True)
    i1 = jnp.min(jnp.where(logits == m1, lane, float(LANE)), axis=-1, keepdims=True)
    rest = jnp.where(lane == i1, -jnp.inf, logits)
    m2 = jnp.max(rest, axis=-1, keepdims=True)
    i2 = jnp.min(jnp.where(rest == m2, lane, float(LANE)), axis=-1, keepdims=True)
    e = jnp.exp(m2 - m1)
    w1 = 1.0 / (1.0 + e)
    w2 = e * w1
    idx_ref[...] = jnp.where(lane_i == 0, i1, jnp.where(lane_i == 1, i2, 0.0)).astype(jnp.int32)
    wts_ref[...] = jnp.where(lane_i == 0, w1, jnp.where(lane_i == 1, w2, 0.0))


def _router(x, nw, rw_pad, rb_pad, *, tm):
    t = x.shape[0]
    return pl.pallas_call(
        _router_kernel,
        grid=(t // tm,),
        in_specs=[
            pl.BlockSpec((tm, D_MODEL), lambda i: (i, 0)),
            pl.BlockSpec((1, D_MODEL), lambda i: (0, 0)),
            pl.BlockSpec((D_MODEL, LANE), lambda i: (0, 0)),
            pl.BlockSpec((1, LANE), lambda i: (0, 0)),
        ],
        out_specs=[
            pl.BlockSpec((tm, D_MODEL), lambda i: (i, 0)),
            pl.BlockSpec((tm, LANE), lambda i: (i, 0)),
            pl.BlockSpec((tm, LANE), lambda i: (i, 0)),
        ],
        out_shape=[
            jax.ShapeDtypeStruct((t, D_MODEL), F32),
            jax.ShapeDtypeStruct((t, LANE), jnp.int32),
            jax.ShapeDtypeStruct((t, LANE), F32),
        ],
        compiler_params=_cparams(("parallel",)),
        name="router",
    )(x, nw, rw_pad, rb_pad)


def _dispatch_kernel(dest_ref, pad_start_ref, pad_count_ref, h_ref, xs_hbm, zero_scr, sem, *, nb, n_tok):
    step = pl.program_id(0)
    n_tok_steps = n_tok // nb

    def row_copy(src_ref, src_row, dst_row):
        return pltpu.make_async_copy(src_ref.at[pl.ds(src_row, 1), :],
                                     xs_hbm.at[pl.ds(dst_row, 1), :], sem.at[0])

    @pl.when(step < n_tok_steps)
    def _():
        base = step * (nb * TOP_K)

        def issue(r, c):
            for kk in range(TOP_K):
                row_copy(h_ref, r, dest_ref[base + r * TOP_K + kk]).start()
            return c

        lax.fori_loop(0, nb, issue, 0, unroll=4)

        for kk in range(TOP_K):
            pltpu.make_async_copy(h_ref, xs_hbm.at[pl.ds(0, nb), :], sem.at[0]).wait()

    @pl.when(step >= n_tok_steps)
    def _():
        e = step - n_tok_steps
        zero_scr[...] = jnp.zeros_like(zero_scr)
        start = pad_start_ref[e]
        n = pad_count_ref[e]

        def issue(r, c):
            row_copy(zero_scr, 0, start + r).start()
            return c

        lax.fori_loop(0, n, issue, 0)

        def drain(r, c):
            row_copy(zero_scr, 0, 0).wait()
            return c

        lax.fori_loop(0, n, drain, 0)


def _dispatch(dest, pad_start, pad_count, h, t_pad, *, nb):
    n_tok = h.shape[0]
    n_tok_steps = n_tok // nb
    kern = functools.partial(_dispatch_kernel, nb=nb, n_tok=n_tok)
    grid_spec = pltpu.PrefetchScalarGridSpec(
        num_scalar_prefetch=3,
        grid=(n_tok_steps + N_EXPERTS,),
        in_specs=[pl.BlockSpec((nb, D_MODEL),
                               lambda i, d, ps, pc: (jnp.minimum(i, n_tok_steps - 1), 0))],
        out_specs=pl.BlockSpec(memory_space=pl.ANY),
        scratch_shapes=[pltpu.VMEM((8, D_MODEL), F32), pltpu.SemaphoreType.DMA((1,))],
    )
    return pl.pallas_call(
        kern,
        grid_spec=grid_spec,
        out_shape=jax.ShapeDtypeStruct((t_pad, D_MODEL), F32),
        compiler_params=_cparams(("arbitrary",)),
        name="dispatch",
    )(dest, pad_start, pad_count, h)


def _moe_ffn_kernel(te_ref, tv_ref, tb_ref, xs_hbm, wg_ref, wu_ref, wd_ref, ys_hbm,
                    x_scr, acc_scr, wg_scr, wu_scr, wd_scr, a_scr, pend_ref, sem_in, sem_out, *, sub, tm):
    i = pl.program_id(0)
    j = pl.program_id(1)
    last_i = pl.num_programs(0) - 1
    last_j = pl.num_programs(1) - 1
    valid = tv_ref[i]
    nsub = (valid + sub - 1) // sub
    tile_row0 = tb_ref[i] * tm

    def in_copy(s):
        r = pl.multiple_of(s * sub, sub)
        return pltpu.make_async_copy(xs_hbm.at[pl.ds(tile_row0 + r, sub), :],
                                     x_scr.at[pl.ds(r, sub), :], sem_in.at[0])

    def out_copy(s):
        r = pl.multiple_of(s * sub, sub)
        return pltpu.make_async_copy(acc_scr.at[pl.ds(r, sub), :],
                                     ys_hbm.at[pl.ds(tile_row0 + r, sub), :], sem_out.at[0])

    def wait_pending_out():
        def body(s, c):
            out_copy(0).wait()
            return c

        lax.fori_loop(0, pend_ref[0], body, 0)
        pend_ref[0] = 0

    @pl.when((i == 0) & (j == 0))
    def _():
        pend_ref[0] = 0

    @pl.when(valid > 0)
    def _():
        @pl.when(j == 0)
        def _():
            wait_pending_out()

            def start_in(s, c):
                in_copy(s).start()
                return c

            lax.fori_loop(0, nsub, start_in, 0)

            def zero_rows(s, c):
                acc_scr[pl.ds(pl.multiple_of(s * sub, sub), sub), :] = jnp.zeros((sub, D_MODEL), F32)
                return c

            lax.fori_loop(0, nsub, zero_rows, 0)

            def wait_in(s, c):
                in_copy(0).wait()
                return c

            lax.fori_loop(0, nsub, wait_in, 0)

        def cast_weights():
            wg_scr[...] = wg_ref[...].astype(BF16)
            wu_scr[...] = wu_ref[...].astype(BF16)
            wd_scr[...] = wd_ref[...].astype(BF16)

        def up(row0):
            h = x_scr[pl.ds(row0, sub), :].astype(BF16)
            g = _bdot(h, wg_scr[...])
            u = _bdot(h, wu_scr[...])
            return (g * _sigmoid(g) * u).astype(BF16)

        def down(row0, a):
            acc_scr[pl.ds(row0, sub), :] += _bdot(a, wd_scr[...])

        npair = nsub // 2
        blk = 2 * sub

        @pl.when(npair == 0)
        def _():
            cast_weights()

        @pl.when(npair > 0)
        def _():
            cast_weights()
            a_scr[0] = up(0)
            a_scr[1] = up(sub)

            def pair_body(p, c):
                base = pl.multiple_of(p * blk, blk)
                prev0 = a_scr[0]
                prev1 = a_scr[1]
                next0 = up(base)
                next1 = up(base + sub)
                down(base - blk, prev0)
                down(base - sub, prev1)
                a_scr[0] = next0
                a_scr[1] = next1
                return c

            lax.fori_loop(1, npair, pair_body, 0)
            tail = pl.multiple_of((npair - 1) * blk, blk)
            down(tail, a_scr[0])
            down(tail + sub, a_scr[1])

        @pl.when(nsub % 2 == 1)
        def _():
            last = pl.multiple_of((nsub - 1) * sub, sub)
            down(last, up(last))

        @pl.when(j == last_j)
        def _():
            def start_out(s, c):
                out_copy(s).start()
                return c

            lax.fori_loop(0, nsub, start_out, 0)
            pend_ref[0] = nsub

    @pl.when((i == last_i) & (j == last_j))
    def _():
        wait_pending_out()


def _moe_ffn(tile_expert, tile_valid, tile_blk, xs, wg, wu, wd, idx, *, tm, tf, sub):
    t_pad = xs.shape[0]
    n_tiles = t_pad // tm
    nf = D_FF_EXPERT // tf
    kern = functools.partial(_moe_ffn_kernel, sub=sub, tm=tm)

    def jj(i, j, tv):
        return jnp.where(tv[i] > 0, j, nf - 1)

    grid_spec = pltpu.PrefetchScalarGridSpec(
        num_scalar_prefetch=3,
        grid=(n_tiles, nf),
        in_specs=[
            pl.BlockSpec(memory_space=pl.ANY),
            pl.BlockSpec((None, None, D_MODEL, tf), lambda i, j, te, tv, tb: (idx, te[i], 0, jj(i, j, tv))),
            pl.BlockSpec((None, None, D_MODEL, tf), lambda i, j, te, tv, tb: (idx, te[i], 0, jj(i, j, tv))),
            pl.BlockSpec((None, None, tf, D_MODEL), lambda i, j, te, tv, tb: (idx, te[i], jj(i, j, tv), 0)),
        ],
        out_specs=pl.BlockSpec(memory_space=pl.ANY),
        scratch_shapes=[
            pltpu.VMEM((tm, D_MODEL), F32),
            pltpu.VMEM((tm, D_MODEL), F32),
            pltpu.VMEM((D_MODEL, tf), BF16),
            pltpu.VMEM((D_MODEL, tf), BF16),
            pltpu.VMEM((tf, D_MODEL), BF16),
            pltpu.VMEM((2, sub, tf), BF16),
            pltpu.SMEM((1,), jnp.int32),
            pltpu.SemaphoreType.DMA((1,)),
            pltpu.SemaphoreType.DMA((1,)),
        ],
    )
    return pl.pallas_call(
        kern,
        grid_spec=grid_spec,
        out_shape=jax.ShapeDtypeStruct((t_pad, D_MODEL), F32),
        compiler_params=_cparams(("arbitrary", "arbitrary")),
        name="moe_ffn",
    )(tile_expert, tile_valid, tile_blk, xs, wg, wu, wd)


def _combine_kernel(dest_ref, x_ref, wts_ref, fw_ref, ys_hbm, o_ref, ya_scr, yb_scr, sem,
                    *, nb, final_norm):
    step = pl.program_id(0)
    nsteps = pl.num_programs(0)
    slot = step % 2

    def row_copy(src_row, dst_ref, slot_, r):
        return pltpu.make_async_copy(ys_hbm.at[pl.ds(src_row, 1), :],
                                     dst_ref.at[slot_, pl.ds(r, 1), :], sem.at[slot_])

    def gather(block, slot_):
        def issue(r, c):
            s = (block * nb + r) * TOP_K
            row_copy(dest_ref[s], ya_scr, slot_, r).start()
            row_copy(dest_ref[s + 1], yb_scr, slot_, r).start()
            return c

        lax.fori_loop(0, nb, issue, 0, unroll=4)

    @pl.when(step == 0)
    def _():
        gather(0, 0)

    @pl.when(step + 1 < nsteps)
    def _():
        gather(step + 1, 1 - slot)

    for buf in (ya_scr, yb_scr):
        pltpu.make_async_copy(ys_hbm.at[pl.ds(0, nb), :], buf.at[slot], sem.at[slot]).wait()

    w = wts_ref[...]
    out = x_ref[...] + w[:, 0:1] * ya_scr[slot] + w[:, 1:2] * yb_scr[slot]
    if final_norm:
        out = _rms(out, fw_ref[...])
    o_ref[...] = out


def _combine(dest, x, wts, fw, ys, *, nb, final_norm):
    t = x.shape[0]
    kern = functools.partial(_combine_kernel, nb=nb, final_norm=final_norm)
    grid_spec = pltpu.PrefetchScalarGridSpec(
        num_scalar_prefetch=1,
        grid=(t // nb,),
        in_specs=[
            pl.BlockSpec((nb, D_MODEL), lambda i, d: (i, 0)),
            pl.BlockSpec((nb, LANE), lambda i, d: (i, 0)),
            pl.BlockSpec((1, D_MODEL), lambda i, d: (0, 0)),
            pl.BlockSpec(memory_space=pl.ANY),
        ],
        out_specs=pl.BlockSpec((nb, D_MODEL), lambda i, d: (i, 0)),
        scratch_shapes=[
            pltpu.VMEM((2, nb, D_MODEL), F32),
            pltpu.VMEM((2, nb, D_MODEL), F32),
            pltpu.SemaphoreType.DMA((2,)),
        ],
    )
    return pl.pallas_call(
        kern,
        grid_spec=grid_spec,
        out_shape=jax.ShapeDtypeStruct((t, D_MODEL), F32),
        compiler_params=_cparams(("arbitrary",)),
        name="combine",
    )(dest, x, wts, fw, ys)


def _routing_tables(idx2, tm, sub):
    n_slots = idx2.shape[0] * TOP_K
    e_flat = idx2.reshape(-1)
    onehot = (e_flat[:, None] == jnp.arange(N_EXPERTS, dtype=jnp.int32)[None, :]).astype(jnp.int32)
    csum = jnp.cumsum(onehot, axis=0)
    counts = csum[-1]
    rank = jnp.sum(onehot * csum, axis=1) - 1
    padded = ((counts + tm - 1) // tm) * tm
    gend = jnp.cumsum(padded)
    gstart = gend - padded
    dest = (jnp.sum(onehot * gstart[None, :], axis=1) + rank).astype(jnp.int32)

    n_tiles = n_slots // tm + N_EXPERTS
    t_pad = n_tiles * tm
    tile_row0 = jnp.arange(n_tiles, dtype=jnp.int32) * tm
    n_used = gend[-1] // tm
    tile_blk = jnp.minimum(jnp.arange(n_tiles, dtype=jnp.int32), n_used - 1)
    blk_row0 = tile_blk * tm
    tile_expert = jnp.sum((blk_row0[:, None] >= gend[None, :]).astype(jnp.int32), axis=1)
    tile_expert = jnp.minimum(tile_expert, N_EXPERTS - 1)
    valid = jnp.clip(gstart[tile_expert] + counts[tile_expert] - tile_row0, 0, tm)
    tile_valid = jnp.where(jnp.arange(n_tiles) < n_used, valid, 0).astype(jnp.int32)

    pad_start = (gstart + counts).astype(jnp.int32)
    pad_count = (((counts + sub - 1) // sub) * sub - counts).astype(jnp.int32)
    return (dest, pad_start, pad_count, tile_expert.astype(jnp.int32), tile_valid,
            tile_blk.astype(jnp.int32), t_pad)


def _moe_block(x, nw, rw, rb, wg, wu, wd, idx, fw, final_norm):
    t = x.shape[0]
    tm = min(TM, t)
    rw_pad = jnp.pad(rw, ((0, 0), (0, LANE - N_EXPERTS)))
    rb_pad = jnp.pad(rb, (0, LANE - N_EXPERTS), constant_values=-1e30).reshape(1, LANE)
    moe_tm = min(MOE_TM, TOP_K * t)
    h, idx_l, wts = _router(x, nw, rw_pad, rb_pad, tm=tm)
    dest, pad_start, pad_count, tile_expert, tile_valid, tile_blk, t_pad = _routing_tables(
        idx_l[:, :TOP_K], moe_tm, MOE_SUB)
    xs = _dispatch(dest, pad_start, pad_count, h, t_pad, nb=ROW_BATCH)
    ys = _moe_ffn(tile_expert, tile_valid, tile_blk, xs, wg, wu, wd, idx, tm=moe_tm, tf=TF_MOE,
                  sub=MOE_SUB)
    return _combine(dest, x, wts, fw, ys, nb=ROW_BATCH, final_norm=final_norm)


def _final_norm_kernel(x_ref, w_ref, o_ref):
    o_ref[...] = _rms(x_ref[...], w_ref[...])


def _final_norm(x, w, *, tm):
    t = x.shape[0]
    return pl.pallas_call(
        _final_norm_kernel,
        grid=(t // tm,),
        in_specs=[pl.BlockSpec((tm, D_MODEL), lambda i: (i, 0)),
                  pl.BlockSpec((1, D_MODEL), lambda i: (0, 0))],
        out_specs=pl.BlockSpec((tm, D_MODEL), lambda i: (i, 0)),
        out_shape=jax.ShapeDtypeStruct((t, D_MODEL), F32),
        compiler_params=_cparams(("parallel",)),
        name="final_norm",
    )(x, w)


def _mixer(x, h, layer, w_in, lower_bound_logits, gk_up_w, gk_up_b, hgrn_norm_w, gla_norm_w,
           w_proj_a, w_proj_b, w_out, consts):
    t = x.shape[0]
    tm = min(TM_PROJ, t)
    tb = min(ATT_TB, t)
    bounds = jnp.cumsum(jax.nn.softmax(lower_bound_logits.astype(F32), axis=0), axis=0)
    lb = (bounds[layer] - bounds[0]).reshape(1, HGRN_KEY)
    u, logf = _inproj_main(h, w_in, layer, lb, tm=tm, tn=TN_IN)
    gates, loggk = _inproj_tail(h, w_in, layer, gk_up_w[layer], gk_up_b[layer].reshape(1, GLA_KEY),
                                tm=tm, tn=TN_IN)
    hb_a, hb_b = 4, 4
    wa = hb_a * HEAD_K
    y_a = _recurrence(u, 0, u, 1024 // wa, logf, u, 2048 // wa, u, 3072 // wa,
                      hgrn_norm_w[layer].reshape(1, -1), consts,
                      heads=HGRN_HEADS, dv=128, hb=hb_a, tb=tb, c=ATT_CHUNK)
    wkb, wvb = hb_b * HEAD_K, hb_b * 256
    y_b = _recurrence(u, 4096 // wkb, u, 4608 // wkb, loggk, u, 5120 // wvb, u, 6144 // wvb,
                      gla_norm_w[layer].reshape(1, -1), consts,
                      heads=GLA_HEADS, dv=256, hb=hb_b, tb=tb, c=ATT_CHUNK)
    merged = _merge(y_a, y_b, w_proj_a, w_proj_b, layer, gates, tm=tm, tn=TN_OUT)
    return _outproj(merged, w_out, layer, x, tm=tm, tn=TN_OUT)


def kernel(x, mix_norm_w, w_in, lower_bound_logits, gk_up_w, gk_up_b, hgrn_norm_w, gla_norm_w, w_proj_a,
           w_proj_b, w_out, ffn_norm_w, dense_w_gate, dense_w_up, dense_w_down, router_w, router_b,
           moe_w_gate, moe_w_up, moe_w_down, final_norm_w):
    b, t, d = x.shape
    xs = x.reshape(b * t, d)
    consts = _recurrence_constants(ATT_CHUNK)
    fw = final_norm_w.reshape(1, D_MODEL)
    fused_final = False
    w_in_t = jnp.swapaxes(w_in, 1, 2)
    tm = min(TM, xs.shape[0])
    h_mix = None
    for layer in range(DEPTH):
        if h_mix is None:
            h_mix = _norm_cast(xs, mix_norm_w[layer].reshape(1, D_MODEL), tm=tm)
        xs = _mixer(xs, h_mix, layer, w_in_t, lower_bound_logits, gk_up_w, gk_up_b, hgrn_norm_w,
                    gla_norm_w, w_proj_a, w_proj_b, w_out, consts)
        h_mix = None
        nw = ffn_norm_w[layer].reshape(1, D_MODEL)
        j = layer // 2
        if layer % 2 == 0:
            nw_next = mix_norm_w[min(layer + 1, DEPTH - 1)].reshape(1, D_MODEL)
            xs, h_next = _dense_ffn(xs, nw, nw_next, dense_w_gate, dense_w_up, dense_w_down, j,
                                    tm=tm, tf=TF_DENSE)
            if layer + 1 < DEPTH:
                h_mix = h_next
        else:
            fused_final = layer == DEPTH - 1
            xs = _moe_block(xs, nw, router_w[j], router_b[j], moe_w_gate, moe_w_up, moe_w_down, j,
                            fw, fused_final)
    if not fused_final:
        xs = _final_norm(xs, fw, tm=min(TM, xs.shape[0]))
    return xs.reshape(b, t, d)
```

```python
import functools

import numpy as np
import jax
import jax.numpy as jnp
from jax import lax
from jax.experimental import pallas as pl
from jax.experimental.pallas import tpu as pltpu

F32 = jnp.float32
BF16 = jnp.bfloat16

D_MODEL = 2048
DEPTH = 2
HGRN_HEADS = 8
HGRN_KEY = 1024
HGRN_VAL = 1024
GLA_HEADS = 4
GLA_KEY = 512
GLA_VAL = 1024
HEAD_K = 128
GLA_GATE_RANK = 16
GLA_GATE_NORMALIZER = 16.0
D_FF_DENSE = 5632
N_EXPERTS = 8
TOP_K = 2
D_FF_EXPERT = 7168
EPS = 1e-6

MAIN_COLS = 4 * 1024 + 2 * 512 + 2 * 1024
GATE_COLS = 2 * D_MODEL

LANE = 128
VMEM_LIMIT = 56 * 1024 * 1024

TM = 1024
TN_IN = 512
TN_OUT = 512
TF_DENSE = 256
TF_MOE = 256
MOE_TM = 2304
MOE_SUB = 256
ATT_CHUNK = 128
ATT_TB = 1024
ROW_BATCH = 256
ROW_CHUNKS = 4
TM_PROJ = 2048


def _cparams(sem):
    return pltpu.CompilerParams(dimension_semantics=sem, vmem_limit_bytes=VMEM_LIMIT)


def _rms(x, w):
    return x * lax.rsqrt(jnp.mean(x * x, axis=-1, keepdims=True) + EPS) * w


def _sigmoid(x):
    return 0.5 * jnp.tanh(0.5 * x) + 0.5


def _log_sigmoid(x):
    return jnp.minimum(x, 0.0) - jnp.log1p(jnp.exp(-jnp.abs(x)))


def _bdot(a, b):
    return jnp.dot(a, b, preferred_element_type=F32)


_NT = (((1,), (1,)), ((), ()))
_TN = (((0,), (0,)), ((), ()))


def _bdot_nt(a, b):
    return lax.dot_general(a, b, _NT, preferred_element_type=F32)


def _row_chunks(tm):
    step = max(tm // ROW_CHUNKS, 8)
    return [slice(r, r + step) for r in range(0, tm, step)]


def _single_buffered(block_shape, index_map):
    return pl.BlockSpec(block_shape, index_map, pipeline_mode=pl.Buffered(1))


def _norm_cast_kernel(x_ref, w_ref, h_ref):
    h_ref[...] = _rms(x_ref[...], w_ref[...]).astype(BF16)


def _norm_cast(x, w, *, tm):
    t = x.shape[0]
    return pl.pallas_call(
        _norm_cast_kernel,
        grid=(t // tm,),
        in_specs=[pl.BlockSpec((tm, D_MODEL), lambda i: (i, 0)),
                  pl.BlockSpec((1, D_MODEL), lambda i: (0, 0))],
        out_specs=pl.BlockSpec((tm, D_MODEL), lambda i: (i, 0)),
        out_shape=jax.ShapeDtypeStruct((t, D_MODEL), BF16),
        compiler_params=_cparams(("parallel",)),
        name="norm_cast",
    )(x, w)


def _inproj_main_kernel(h_ref, w_ref, lb_ref, u_ref, logf_ref, *, tn):
    j = pl.program_id(1)
    chunks = _row_chunks(h_ref.shape[0])

    def project(rows):
        return _bdot_nt(h_ref[rows, :], w_ref[...].astype(BF16))

    per = 1024 // tn
    half = 512 // tn
    scale = HEAD_K ** -0.5
    b_f, b_i, b_g = per, 2 * per, 3 * per
    b_qb = 4 * per
    b_kb = b_qb + half
    b_vb = b_kb + half
    b_gb = b_vb + per

    is_q = (j < b_f) | ((j >= b_qb) & (j < b_kb))
    is_f = (j >= b_f) & (j < b_i)
    is_gate = ((j >= b_g) & (j < b_qb)) | (j >= b_gb)
    is_plain = jnp.logical_not(is_q | is_f | is_gate)

    @pl.when(is_q)
    def _():
        for rows in chunks:
            u_ref[rows, :] = (project(rows) * scale).astype(BF16)

    @pl.when(is_plain)
    def _():
        for rows in chunks:
            u_ref[rows, :] = project(rows).astype(BF16)

    @pl.when(is_gate)
    def _():
        for rows in chunks:
            acc = project(rows)
            u_ref[rows, :] = (acc * _sigmoid(acc)).astype(BF16)

    @pl.when(is_f)
    def _():
        lb = lb_ref[...]
        log_lb = jnp.log(lb)
        log_1m_lb = jnp.log1p(-lb)
        for rows in chunks:
            acc = project(rows)
            e = jnp.exp(-jnp.abs(acc))
            r = 1.0 / (1.0 + e)
            sig_neg = jnp.where(acc >= 0, e * r, r)
            ls = jnp.minimum(acc, 0.0) - jnp.log1p(e)
            b = log_1m_lb + ls
            m = jnp.maximum(log_lb, b)
            logf_ref[rows, :] = m + jnp.log1p(jnp.exp(-jnp.abs(log_lb - b)))
            u_ref[rows, :] = ((1.0 - lb) * sig_neg).astype(BF16)


def _inproj_main(h, w_in, layer, lb, *, tm, tn):
    t = h.shape[0]
    per = 1024 // tn
    nj = MAIN_COLS // tn
    kern = functools.partial(_inproj_main_kernel, tn=tn)
    return pl.pallas_call(
        kern,
        grid=(t // tm, nj),
        in_specs=[
            pl.BlockSpec((tm, D_MODEL), lambda i, j: (i, 0)),
            pl.BlockSpec((None, tn, D_MODEL), lambda i, j: (layer, j, 0)),
            pl.BlockSpec((1, tn), lambda i, j: (0, jnp.clip(j - per, 0, per - 1))),
        ],
        out_specs=[
            pl.BlockSpec((tm, tn), lambda i, j: (i, j)),
            pl.BlockSpec((tm, tn), lambda i, j: (i, jnp.clip(j - per, 0, per - 1))),
        ],
        out_shape=[
            jax.ShapeDtypeStruct((t, MAIN_COLS), BF16),
            jax.ShapeDtypeStruct((t, HGRN_KEY), F32),
        ],
        compiler_params=_cparams(("parallel", "arbitrary")),
        name="inproj_main",
    )(h, w_in, lb)


def _inproj_tail_kernel(h_ref, wm_ref, wgk_ref, up_ref, upb_ref, gates_ref, loggk_ref):
    j = pl.program_id(1)
    chunks = _row_chunks(h_ref.shape[0])

    @pl.when(j == 0)
    def _():
        wgk = wgk_ref[0].astype(BF16)
        up = up_ref[...].astype(BF16)
        for rows in chunks:
            code = _bdot_nt(h_ref[rows, :], wgk)
            lg = _bdot(code.astype(BF16), up) + upb_ref[...]
            loggk_ref[rows, :] = _log_sigmoid(lg) * (1.0 / GLA_GATE_NORMALIZER)

    for rows in chunks:
        acc = _bdot_nt(h_ref[rows, :], wm_ref[0].astype(BF16))
        gates_ref[rows, :] = _sigmoid(acc).astype(BF16)


def _inproj_tail(h, w_in, layer, up_w, up_b, *, tm, tn):
    t = h.shape[0]
    gate_row0 = MAIN_COLS + GLA_GATE_RANK
    return pl.pallas_call(
        _inproj_tail_kernel,
        grid=(t // tm, GATE_COLS // tn),
        in_specs=[
            pl.BlockSpec((tm, D_MODEL), lambda i, j: (i, 0)),
            pl.BlockSpec((pl.Element(1), pl.Element(tn), pl.Element(D_MODEL)),
                         lambda i, j: (layer, pl.multiple_of(gate_row0 + j * tn, 8), 0)),
            pl.BlockSpec((pl.Element(1), pl.Element(GLA_GATE_RANK), pl.Element(D_MODEL)),
                         lambda i, j: (layer, MAIN_COLS, 0)),
            pl.BlockSpec((GLA_GATE_RANK, GLA_KEY), lambda i, j: (0, 0)),
            pl.BlockSpec((1, GLA_KEY), lambda i, j: (0, 0)),
        ],
        out_specs=[
            pl.BlockSpec((tm, tn), lambda i, j: (i, j)),
            pl.BlockSpec((tm, GLA_KEY), lambda i, j: (i, 0)),
        ],
        out_shape=[
            jax.ShapeDtypeStruct((t, GATE_COLS), BF16),
            jax.ShapeDtypeStruct((t, GLA_KEY), F32),
        ],
        compiler_params=_cparams(("parallel", "arbitrary")),
        name="inproj_tail",
    )(h, w_in, w_in, up_w, up_b)


def _recurrence_constants(c):
    levels = int(np.log2(c))
    t = np.arange(c)[:, None]
    s = np.arange(c)[None, :]
    mats = [(s <= t), (s > t)]
    masks = [np.eye(c, dtype=bool)]
    for l in range(levels):
        b = 1 << l
        tb = (t // (2 * b)) * 2 * b + b - 1
        is_i = (t // b) % 2 == 1
        mats.append(np.where(is_i, (s > tb) & (s <= t), (s > t) & (s <= tb)))
        i = np.arange(c)[:, None]
        j = np.arange(c)[None, :]
        masks.append((i // (2 * b) == j // (2 * b)) & ((i // b) % 2 == 1) & ((j // b) % 2 == 0))
    m = np.concatenate(mats, axis=0).astype(np.float32)
    m2 = np.concatenate([m, m], axis=1)
    masks2 = np.tile(np.stack(masks).astype(np.float32), (1, 1, 2))
    return jnp.asarray(m2, BF16), jnp.asarray(masks2)


def _recurrence_kernel(q_ref, k_ref, g_ref, v_ref, sg_ref, nw_ref, m2_ref, masks_ref, y_ref, st_ref,
                       *, c, dk, dv, nchunk, npairs):
    levels = int(np.log2(c))
    w = 2 * dk

    @pl.when(pl.program_id(1) == 0)
    def _():
        st_ref[...] = jnp.zeros_like(st_ref)

    row = lax.broadcasted_iota(jnp.int32, (c, w), 0)
    shift = int(np.log2(dk))
    same_head = ((lax.broadcasted_iota(jnp.int32, (w, w), 0) >> shift)
                 == (lax.broadcasted_iota(jnp.int32, (w, w), 1) >> shift))
    nw = nw_ref[...]

    def block_diag(b):
        return jnp.where(same_head, jnp.concatenate([b, b], axis=0), jnp.zeros((), b.dtype))

    def pair_nt(a, b):
        return lax.dot_general(a, block_diag(b), _NT, preferred_element_type=F32)

    def chunk_body(ci, carry):
        r0 = pl.multiple_of(ci * c, c)
        rows = pl.ds(r0, c)
        for pp in range(npairs):
            pair_chunk(rows, pp)
        return carry

    def pair_chunk(rows, pp):
        kcols = slice(pp * w, (pp + 1) * w)
        vbase = pp * 2 * dv
        g = g_ref[rows, kcols]
        g_hi = g.astype(BF16)
        g_lo = (g - g_hi.astype(F32)).astype(BF16)
        d_all = _bdot(m2_ref[...], jnp.concatenate([g_hi, g_lo], axis=0))
        e_all = jnp.exp(d_all)
        q = q_ref[rows, kcols]
        k = k_ref[rows, kcols]
        qf = q.astype(F32)
        kf = k.astype(F32)
        e_cum = e_all[0:c]
        q_in = (qf * e_cum).astype(BF16)
        k_out = (kf * e_all[c:2 * c]).astype(BF16)
        scores = pair_nt(q, k) * masks_ref[0]
        for l in range(levels):
            b = 1 << l
            if b >= 8:
                parts = [(qf if (r // b) % 2 == 1 else kf)[r:r + b] for r in range(0, c, b)]
                qk = jnp.concatenate(parts, axis=0)
            else:
                qk = jnp.where(((row >> l) & 1) == 1, qf, kf)
            x = (qk * e_all[(l + 2) * c:(l + 3) * c]).astype(BF16)
            scores = scores + pair_nt(x, x) * masks_ref[l + 1]
        sb = scores.astype(BF16)
        v = v_ref[rows, vbase:vbase + 2 * dv]
        decay = e_cum[c - 1:c, :]
        if dv == dk:
            st = st_ref[pp]
            o = _bdot(sb, block_diag(v)) + lax.dot_general(
                q_in, st.astype(BF16), _NT, preferred_element_type=F32)
            upd = lax.dot_general(v, k_out, _TN, preferred_element_type=F32)
            st_ref[pp] = st * decay + jnp.where(same_head, upd, 0.0)
            outs = [o[:, :dv], o[:, dv:]]
        else:
            outs = []
            for hh in range(2):
                ks = slice(hh * dk, (hh + 1) * dk)
                vh = v[:, hh * dv:(hh + 1) * dv]
                st = st_ref[2 * pp + hh]
                outs.append(_bdot(sb[:, ks], vh) + lax.dot_general(
                    q_in[:, ks], st.astype(BF16), _NT, preferred_element_type=F32))
                st_ref[2 * pp + hh] = st * decay[:, ks] + lax.dot_general(
                    vh, k_out[:, ks], _TN, preferred_element_type=F32)
        for hh, o in enumerate(outs):
            vs = slice(vbase + hh * dv, vbase + (hh + 1) * dv)
            o = o * lax.rsqrt(jnp.mean(o * o, axis=-1, keepdims=True) + EPS) * nw
            y_ref[rows, vs] = (o * sg_ref[rows, vs].astype(F32)).astype(BF16)

    lax.fori_loop(0, nchunk, chunk_body, 0)


def _recurrence(q_arr, q_off, k_arr, k_off, g_arr, v_arr, v_off, sg_arr, sg_off, nw, consts,
                *, heads, dv, hb, tb, c):
    t = q_arr.shape[0]
    dk = HEAD_K
    assert hb % 2 == 0 and c == dk
    npairs = hb // 2
    m2, masks = consts
    kern = functools.partial(_recurrence_kernel, c=c, dk=dk, dv=dv, nchunk=tb // c, npairs=npairs)
    wk, wv = hb * dk, hb * dv
    st_shape = (npairs, 2 * dv, 2 * dk) if dv == dk else (hb, dv, dk)
    return pl.pallas_call(
        kern,
        grid=(heads // hb, t // tb),
        in_specs=[
            pl.BlockSpec((tb, wk), lambda h, i: (i, q_off + h)),
            pl.BlockSpec((tb, wk), lambda h, i: (i, k_off + h)),
            pl.BlockSpec((tb, wk), lambda h, i: (i, h)),
            pl.BlockSpec((tb, wv), lambda h, i: (i, v_off + h)),
            pl.BlockSpec((tb, wv), lambda h, i: (i, sg_off + h)),
            pl.BlockSpec((1, dv), lambda h, i: (0, 0)),
            pl.BlockSpec(m2.shape, lambda h, i: (0, 0)),
            pl.BlockSpec(masks.shape, lambda h, i: (0, 0, 0)),
        ],
        out_specs=pl.BlockSpec((tb, wv), lambda h, i: (i, h)),
        out_shape=jax.ShapeDtypeStruct((t, heads * dv), BF16),
        scratch_shapes=[pltpu.VMEM(st_shape, F32)],
        compiler_params=_cparams(("parallel", "arbitrary")),
        name="recurrence_dv%d" % dv,
    )(q_arr, k_arr, g_arr, v_arr, sg_arr, nw, m2, masks)


def _merge_kernel(ya_ref, yb_ref, wa_ref, wb_ref, ga_ref, gb_ref, o_ref):
    pa = _bdot(ya_ref[...], wa_ref[...].astype(BF16))
    pb = _bdot(yb_ref[...], wb_ref[...].astype(BF16))
    o_ref[...] = (ga_ref[...].astype(F32) * pa + gb_ref[...].astype(F32) * pb).astype(BF16)


def _merge(ya, yb, wa, wb, layer, gates, *, tm, tn):
    t = ya.shape[0]
    nj = D_MODEL // tn
    return pl.pallas_call(
        _merge_kernel,
        grid=(t // tm, nj),
        in_specs=[
            pl.BlockSpec((tm, HGRN_VAL), lambda i, j: (i, 0)),
            pl.BlockSpec((tm, GLA_VAL), lambda i, j: (i, 0)),
            pl.BlockSpec((None, HGRN_VAL, tn), lambda i, j: (layer, 0, j)),
            pl.BlockSpec((None, GLA_VAL, tn), lambda i, j: (layer, 0, j)),
            pl.BlockSpec((tm, tn), lambda i, j: (i, j)),
            pl.BlockSpec((tm, tn), lambda i, j: (i, nj + j)),
        ],
        out_specs=pl.BlockSpec((tm, tn), lambda i, j: (i, j)),
        out_shape=jax.ShapeDtypeStruct((t, D_MODEL), BF16),
        compiler_params=_cparams(("parallel", "arbitrary")),
        name="merge",
    )(ya, yb, wa, wb, gates, gates)


def _outproj_kernel(m_ref, w_ref, x_ref, o_ref):
    o_ref[...] = x_ref[...] + _bdot(m_ref[...], w_ref[...].astype(BF16))


def _outproj(merged, w_out, layer, x, *, tm, tn):
    t = x.shape[0]
    return pl.pallas_call(
        _outproj_kernel,
        grid=(t // tm, D_MODEL // tn),
        in_specs=[
            pl.BlockSpec((tm, D_MODEL), lambda i, j: (i, 0)),
            pl.BlockSpec((None, D_MODEL, tn), lambda i, j: (layer, 0, j)),
            pl.BlockSpec((tm, tn), lambda i, j: (i, j)),
        ],
        out_specs=pl.BlockSpec((tm, tn), lambda i, j: (i, j)),
        out_shape=jax.ShapeDtypeStruct((t, D_MODEL), F32),
        compiler_params=_cparams(("parallel", "arbitrary")),
        name="outproj",
    )(merged, w_out, x)


def _dense_ffn_kernel(x_ref, nw_ref, wg_ref, wu_ref, wd_ref, o_ref, h_scr):
    j = pl.program_id(1)

    @pl.when(j == 0)
    def _():
        x = x_ref[...]
        h_scr[...] = _rms(x, nw_ref[...]).astype(BF16)
        o_ref[...] = x

    h = h_scr[...]
    g = _bdot(h, wg_ref[...].astype(BF16))
    u = _bdot(h, wu_ref[...].astype(BF16))
    a = (g * _sigmoid(g) * u).astype(BF16)
    o_ref[...] += _bdot(a, wd_ref[...].astype(BF16))


def _dense_ffn(x, nw, wg, wu, wd, idx, *, tm, tf):
    t = x.shape[0]
    return pl.pallas_call(
        _dense_ffn_kernel,
        grid=(t // tm, D_FF_DENSE // tf),
        in_specs=[
            pl.BlockSpec((tm, D_MODEL), lambda i, j: (i, 0)),
            pl.BlockSpec((1, D_MODEL), lambda i, j: (0, 0)),
            pl.BlockSpec((None, D_MODEL, tf), lambda i, j: (idx, 0, j)),
            pl.BlockSpec((None, D_MODEL, tf), lambda i, j: (idx, 0, j)),
            pl.BlockSpec((None, tf, D_MODEL), lambda i, j: (idx, j, 0)),
        ],
        out_specs=pl.BlockSpec((tm, D_MODEL), lambda i, j: (i, 0)),
        out_shape=jax.ShapeDtypeStruct((t, D_MODEL), F32),
        scratch_shapes=[pltpu.VMEM((tm, D_MODEL), BF16)],
        compiler_params=_cparams(("parallel", "arbitrary")),
        name="dense_ffn",
    )(x, nw, wg, wu, wd)


def _router_kernel(x_ref, nw_ref, rw_ref, rb_ref, h_ref, idx_ref, wts_ref):
    h = _rms(x_ref[...], nw_ref[...])
    h_ref[...] = h
    logits = jnp.dot(h, rw_ref[...], preferred_element_type=F32,
                     precision=lax.Precision.HIGHEST) + rb_ref[...]
    lane_i = lax.broadcasted_iota(jnp.int32, logits.shape, 1)
    lane = lane_i.astype(F32)
    m1 = jnp.max(logits, axis=-1, keepdims=True)
    i1 = jnp.min(jnp.where(logits == m1, lane, float(LANE)), axis=-1, keepdims=True)
    rest = jnp.where(lane == i1, -jnp.inf, logits)
    m2 = jnp.max(rest, axis=-1, keepdims=True)
    i2 = jnp.min(jnp.where(rest == m2, lane, float(LANE)), axis=-1, keepdims=True)
    e = jnp.exp(m2 - m1)
    w1 = 1.0 / (1.0 + e)
    w2 = e * w1
    idx_ref[...] = jnp.where(lane_i == 0, i1, jnp.where(lane_i == 1, i2, 0.0)).astype(jnp.int32)
    wts_ref[...] = jnp.where(lane_i == 0, w1, jnp.where(lane_i == 1, w2, 0.0))


def _router(x, nw, rw_pad, rb_pad, *, tm):
    t = x.shape[0]
    return pl.pallas_call(
        _router_kernel,
        grid=(t // tm,),
        in_specs=[
            pl.BlockSpec((tm, D_MODEL), lambda i: (i, 0)),
            pl.BlockSpec((1, D_MODEL), lambda i: (0, 0)),
            pl.BlockSpec((D_MODEL, LANE), lambda i: (0, 0)),
            pl.BlockSpec((1, LANE), lambda i: (0, 0)),
        ],
        out_specs=[
            pl.BlockSpec((tm, D_MODEL), lambda i: (i, 0)),
            pl.BlockSpec((tm, LANE), lambda i: (i, 0)),
            pl.BlockSpec((tm, LANE), lambda i: (i, 0)),
        ],
        out_shape=[
            jax.ShapeDtypeStruct((t, D_MODEL), F32),
            jax.ShapeDtypeStruct((t, LANE), jnp.int32),
            jax.ShapeDtypeStruct((t, LANE), F32),
        ],
        compiler_params=_cparams(("parallel",)),
        name="router",
    )(x, nw, rw_pad, rb_pad)


def _dispatch_kernel(dest_ref, pad_start_ref, pad_count_ref, h_ref, xs_hbm, zero_scr, sem, *, nb, n_tok):
    step = pl.program_id(0)
    n_tok_steps = n_tok // nb

    def row_copy(src_ref, src_row, dst_row):
        return pltpu.make_async_copy(src_ref.at[pl.ds(src_row, 1), :],
                                     xs_hbm.at[pl.ds(dst_row, 1), :], sem.at[0])

    @pl.when(step < n_tok_steps)
    def _():
        base = step * (nb * TOP_K)

        def issue(r, c):
            for kk in range(TOP_K):
                row_copy(h_ref, r, dest_ref[base + r * TOP_K + kk]).start()
            return c

        lax.fori_loop(0, nb, issue, 0, unroll=4)

        for kk in range(TOP_K):
            pltpu.make_async_copy(h_ref, xs_hbm.at[pl.ds(0, nb), :], sem.at[0]).wait()

    @pl.when(step >= n_tok_steps)
    def _():
        e = step - n_tok_steps
        zero_scr[...] = jnp.zeros_like(zero_scr)
        start = pad_start_ref[e]
        n = pad_count_ref[e]

        def issue(r, c):
            row_copy(zero_scr, 0, start + r).start()
            return c

        lax.fori_loop(0, n, issue, 0)

        def drain(r, c):
            row_copy(zero_scr, 0, 0).wait()
            return c

        lax.fori_loop(0, n, drain, 0)


def _dispatch(dest, pad_start, pad_count, h, t_pad, *, nb):
    n_tok = h.shape[0]
    n_tok_steps = n_tok // nb
    kern = functools.partial(_dispatch_kernel, nb=nb, n_tok=n_tok)
    grid_spec = pltpu.PrefetchScalarGridSpec(
        num_scalar_prefetch=3,
        grid=(n_tok_steps + N_EXPERTS,),
        in_specs=[pl.BlockSpec((nb, D_MODEL),
                               lambda i, d, ps, pc: (jnp.minimum(i, n_tok_steps - 1), 0))],
        out_specs=pl.BlockSpec(memory_space=pl.ANY),
        scratch_shapes=[pltpu.VMEM((8, D_MODEL), F32), pltpu.SemaphoreType.DMA((1,))],
    )
    return pl.pallas_call(
        kern,
        grid_spec=grid_spec,
        out_shape=jax.ShapeDtypeStruct((t_pad, D_MODEL), F32),
        compiler_params=_cparams(("arbitrary",)),
        name="dispatch",
    )(dest, pad_start, pad_count, h)


def _moe_ffn_kernel(te_ref, tv_ref, tb_ref, xs_hbm, wg_ref, wu_ref, wd_ref, ys_hbm,
                    x_scr, acc_scr, wg_scr, wu_scr, wd_scr, a_scr, pend_ref, sem_in, sem_out, *, sub, tm):
    i = pl.program_id(0)
    j = pl.program_id(1)
    last_i = pl.num_programs(0) - 1
    last_j = pl.num_programs(1) - 1
    valid = tv_ref[i]
    nsub = (valid + sub - 1) // sub
    tile_row0 = tb_ref[i] * tm

    def in_copy(s):
        r = pl.multiple_of(s * sub, sub)
        return pltpu.make_async_copy(xs_hbm.at[pl.ds(tile_row0 + r, sub), :],
                                     x_scr.at[pl.ds(r, sub), :], sem_in.at[0])

    def out_copy(s):
        r = pl.multiple_of(s * sub, sub)
        return pltpu.make_async_copy(acc_scr.at[pl.ds(r, sub), :],
                                     ys_hbm.at[pl.ds(tile_row0 + r, sub), :], sem_out.at[0])

    def wait_pending_out():
        def body(s, c):
            out_copy(0).wait()
            return c

        lax.fori_loop(0, pend_ref[0], body, 0)
        pend_ref[0] = 0

    @pl.when((i == 0) & (j == 0))
    def _():
        pend_ref[0] = 0

    @pl.when(valid > 0)
    def _():
        @pl.when(j == 0)
        def _():
            wait_pending_out()

            def start_in(s, c):
                in_copy(s).start()
                return c

            lax.fori_loop(0, nsub, start_in, 0)

            def zero_rows(s, c):
                acc_scr[pl.ds(pl.multiple_of(s * sub, sub), sub), :] = jnp.zeros((sub, D_MODEL), F32)
                return c

            lax.fori_loop(0, nsub, zero_rows, 0)

            def wait_in(s, c):
                in_copy(0).wait()
                return c

            lax.fori_loop(0, nsub, wait_in, 0)

        def cast_weights():
            wg_scr[...] = wg_ref[...].astype(BF16)
            wu_scr[...] = wu_ref[...].astype(BF16)
            wd_scr[...] = wd_ref[...].astype(BF16)

        def up(row0, nrows):
            h = x_scr[pl.ds(row0, nrows), :].astype(BF16)
            g = _bdot(h, wg_scr[...])
            u = _bdot(h, wu_scr[...])
            return (g * _sigmoid(g) * u).astype(BF16)

        def down(row0, nrows, a):
            acc_scr[pl.ds(row0, nrows), :] += _bdot(a, wd_scr[...])

        npair = nsub // 2
        blk = 2 * sub

        @pl.when(npair == 0)
        def _():
            cast_weights()

        @pl.when(npair > 0)
        def _():
            cast_weights()
            a_scr[...] = up(0, blk)

            def pair_body(p, c):
                base = pl.multiple_of(p * blk, blk)
                prev = a_scr[...]
                nxt = up(base, blk)
                down(base - blk, blk, prev)
                a_scr[...] = nxt
                return c

            lax.fori_loop(1, npair, pair_body, 0)
            down(pl.multiple_of((npair - 1) * blk, blk), blk, a_scr[...])

        @pl.when(nsub % 2 == 1)
        def _():
            last = pl.multiple_of((nsub - 1) * sub, sub)
            down(last, sub, up(last, sub))

        @pl.when(j == last_j)
        def _():
            def start_out(s, c):
                out_copy(s).start()
                return c

            lax.fori_loop(0, nsub, start_out, 0)
            pend_ref[0] = nsub

    @pl.when((i == last_i) & (j == last_j))
    def _():
        wait_pending_out()


def _moe_ffn(tile_expert, tile_valid, tile_blk, xs, wg, wu, wd, idx, *, tm, tf, sub):
    t_pad = xs.shape[0]
    n_tiles = t_pad // tm
    nf = D_FF_EXPERT // tf
    kern = functools.partial(_moe_ffn_kernel, sub=sub, tm=tm)

    def jj(i, j, tv):
        return jnp.where(tv[i] > 0, j, nf - 1)

    grid_spec = pltpu.PrefetchScalarGridSpec(
        num_scalar_prefetch=3,
        grid=(n_tiles, nf),
        in_specs=[
            pl.BlockSpec(memory_space=pl.ANY),
            pl.BlockSpec((None, None, D_MODEL, tf), lambda i, j, te, tv, tb: (idx, te[i], 0, jj(i, j, tv))),
            pl.BlockSpec((None, None, D_MODEL, tf), lambda i, j, te, tv, tb: (idx, te[i], 0, jj(i, j, tv))),
            pl.BlockSpec((None, None, tf, D_MODEL), lambda i, j, te, tv, tb: (idx, te[i], jj(i, j, tv), 0)),
        ],
        out_specs=pl.BlockSpec(memory_space=pl.ANY),
        scratch_shapes=[
            pltpu.VMEM((tm, D_MODEL), F32),
            pltpu.VMEM((tm, D_MODEL), F32),
            pltpu.VMEM((D_MODEL, tf), BF16),
            pltpu.VMEM((D_MODEL, tf), BF16),
            pltpu.VMEM((tf, D_MODEL), BF16),
            pltpu.VMEM((2 * sub, tf), BF16),
            pltpu.SMEM((1,), jnp.int32),
            pltpu.SemaphoreType.DMA((1,)),
            pltpu.SemaphoreType.DMA((1,)),
        ],
    )
    return pl.pallas_call(
        kern,
        grid_spec=grid_spec,
        out_shape=jax.ShapeDtypeStruct((t_pad, D_MODEL), F32),
        compiler_params=_cparams(("arbitrary", "arbitrary")),
        name="moe_ffn",
    )(tile_expert, tile_valid, tile_blk, xs, wg, wu, wd)


def _combine_kernel(dest_ref, x_ref, wts_ref, fw_ref, ys_hbm, o_ref, ya_scr, yb_scr, sem,
                    *, nb, final_norm):
    step = pl.program_id(0)
    nsteps = pl.num_programs(0)
    slot = step % 2

    def row_copy(src_row, dst_ref, slot_, r):
        return pltpu.make_async_copy(ys_hbm.at[pl.ds(src_row, 1), :],
                                     dst_ref.at[slot_, pl.ds(r, 1), :], sem.at[slot_])

    def gather(block, slot_):
        def issue(r, c):
            s = (block * nb + r) * TOP_K
            row_copy(dest_ref[s], ya_scr, slot_, r).start()
            row_copy(dest_ref[s + 1], yb_scr, slot_, r).start()
            return c

        lax.fori_loop(0, nb, issue, 0, unroll=4)

    @pl.when(step == 0)
    def _():
        gather(0, 0)

    @pl.when(step + 1 < nsteps)
    def _():
        gather(step + 1, 1 - slot)

    for buf in (ya_scr, yb_scr):
        pltpu.make_async_copy(ys_hbm.at[pl.ds(0, nb), :], buf.at[slot], sem.at[slot]).wait()

    w = wts_ref[...]
    out = x_ref[...] + w[:, 0:1] * ya_scr[slot] + w[:, 1:2] * yb_scr[slot]
    if final_norm:
        out = _rms(out, fw_ref[...])
    o_ref[...] = out


def _combine(dest, x, wts, fw, ys, *, nb, final_norm):
    t = x.shape[0]
    kern = functools.partial(_combine_kernel, nb=nb, final_norm=final_norm)
    grid_spec = pltpu.PrefetchScalarGridSpec(
        num_scalar_prefetch=1,
        grid=(t // nb,),
        in_specs=[
            pl.BlockSpec((nb, D_MODEL), lambda i, d: (i, 0)),
            pl.BlockSpec((nb, LANE), lambda i, d: (i, 0)),
            pl.BlockSpec((1, D_MODEL), lambda i, d: (0, 0)),
            pl.BlockSpec(memory_space=pl.ANY),
        ],
        out_specs=pl.BlockSpec((nb, D_MODEL), lambda i, d: (i, 0)),
        scratch_shapes=[
            pltpu.VMEM((2, nb, D_MODEL), F32),
            pltpu.VMEM((2, nb, D_MODEL), F32),
            pltpu.SemaphoreType.DMA((2,)),
        ],
    )
    return pl.pallas_call(
        kern,
        grid_spec=grid_spec,
        out_shape=jax.ShapeDtypeStruct((t, D_MODEL), F32),
        compiler_params=_cparams(("arbitrary",)),
        name="combine",
    )(dest, x, wts, fw, ys)


def _routing_tables(idx2, tm, sub):
    n_slots = idx2.shape[0] * TOP_K
    e_flat = idx2.reshape(-1)
    onehot = (e_flat[:, None] == jnp.arange(N_EXPERTS, dtype=jnp.int32)[None, :]).astype(jnp.int32)
    csum = jnp.cumsum(onehot, axis=0)
    counts = csum[-1]
    rank = jnp.sum(onehot * csum, axis=1) - 1
    padded = ((counts + tm - 1) // tm) * tm
    gend = jnp.cumsum(padded)
    gstart = gend - padded
    dest = (jnp.sum(onehot * gstart[None, :], axis=1) + rank).astype(jnp.int32)

    n_tiles = n_slots // tm + N_EXPERTS
    t_pad = n_tiles * tm
    tile_row0 = jnp.arange(n_tiles, dtype=jnp.int32) * tm
    n_used = gend[-1] // tm
    tile_blk = jnp.minimum(jnp.arange(n_tiles, dtype=jnp.int32), n_used - 1)
    blk_row0 = tile_blk * tm
    tile_expert = jnp.sum((blk_row0[:, None] >= gend[None, :]).astype(jnp.int32), axis=1)
    tile_expert = jnp.minimum(tile_expert, N_EXPERTS - 1)
    valid = jnp.clip(gstart[tile_expert] + counts[tile_expert] - tile_row0, 0, tm)
    tile_valid = jnp.where(jnp.arange(n_tiles) < n_used, valid, 0).astype(jnp.int32)

    pad_start = (gstart + counts).astype(jnp.int32)
    pad_count = (((counts + sub - 1) // sub) * sub - counts).astype(jnp.int32)
    return (dest, pad_start, pad_count, tile_expert.astype(jnp.int32), tile_valid,
            tile_blk.astype(jnp.int32), t_pad)


def _moe_block(x, nw, rw, rb, wg, wu, wd, idx, fw, final_norm):
    t = x.shape[0]
    tm = min(TM, t)
    rw_pad = jnp.pad(rw, ((0, 0), (0, LANE - N_EXPERTS)))
    rb_pad = jnp.pad(rb, (0, LANE - N_EXPERTS), constant_values=-1e30).reshape(1, LANE)
    moe_tm = min(MOE_TM, TOP_K * t)
    h, idx_l, wts = _router(x, nw, rw_pad, rb_pad, tm=tm)
    dest, pad_start, pad_count, tile_expert, tile_valid, tile_blk, t_pad = _routing_tables(
        idx_l[:, :TOP_K], moe_tm, MOE_SUB)
    xs = _dispatch(dest, pad_start, pad_count, h, t_pad, nb=ROW_BATCH)
    ys = _moe_ffn(tile_expert, tile_valid, tile_blk, xs, wg, wu, wd, idx, tm=moe_tm, tf=TF_MOE,
                  sub=MOE_SUB)
    return _combine(dest, x, wts, fw, ys, nb=ROW_BATCH, final_norm=final_norm)


def _final_norm_kernel(x_ref, w_ref, o_ref):
    o_ref[...] = _rms(x_ref[...], w_ref[...])


def _final_norm(x, w, *, tm):
    t = x.shape[0]
    return pl.pallas_call(
        _final_norm_kernel,
        grid=(t // tm,),
        in_specs=[pl.BlockSpec((tm, D_MODEL), lambda i: (i, 0)),
                  pl.BlockSpec((1, D_MODEL), lambda i: (0, 0))],
        out_specs=pl.BlockSpec((tm, D_MODEL), lambda i: (i, 0)),
        out_shape=jax.ShapeDtypeStruct((t, D_MODEL), F32),
        compiler_params=_cparams(("parallel",)),
        name="final_norm",
    )(x, w)


def _mixer(x, h, layer, w_in, lower_bound_logits, gk_up_w, gk_up_b, hgrn_norm_w, gla_norm_w,
           w_proj_a, w_proj_b, w_out, consts):
    t = x.shape[0]
    tm = min(TM_PROJ, t)
    tb = min(ATT_TB, t)
    bounds = jnp.cumsum(jax.nn.softmax(lower_bound_logits.astype(F32), axis=0), axis=0)
    lb = (bounds[layer] - bounds[0]).reshape(1, HGRN_KEY)
    u, logf = _inproj_main(h, w_in, layer, lb, tm=tm, tn=TN_IN)
    gates, loggk = _inproj_tail(h, w_in, layer, gk_up_w[layer], gk_up_b[layer].reshape(1, GLA_KEY),
                                tm=tm, tn=TN_IN)
    hb_a, hb_b = 4, 4
    wa = hb_a * HEAD_K
    y_a = _recurrence(u, 0, u, 1024 // wa, logf, u, 2048 // wa, u, 3072 // wa,
                      hgrn_norm_w[layer].reshape(1, -1), consts,
                      heads=HGRN_HEADS, dv=128, hb=hb_a, tb=tb, c=ATT_CHUNK)
    wkb, wvb = hb_b * HEAD_K, hb_b * 256
    y_b = _recurrence(u, 4096 // wkb, u, 4608 // wkb, loggk, u, 5120 // wvb, u, 6144 // wvb,
                      gla_norm_w[layer].reshape(1, -1), consts,
                      heads=GLA_HEADS, dv=256, hb=hb_b, tb=tb, c=ATT_CHUNK)
    merged = _merge(y_a, y_b, w_proj_a, w_proj_b, layer, gates, tm=tm, tn=TN_OUT)
    return _outproj(merged, w_out, layer, x, tm=tm, tn=TN_OUT)


def kernel(x, mix_norm_w, w_in, lower_bound_logits, gk_up_w, gk_up_b, hgrn_norm_w, gla_norm_w, w_proj_a,
           w_proj_b, w_out, ffn_norm_w, dense_w_gate, dense_w_up, dense_w_down, router_w, router_b,
           moe_w_gate, moe_w_up, moe_w_down, final_norm_w):
    b, t, d = x.shape
    xs = x.reshape(b * t, d)
    consts = _recurrence_constants(ATT_CHUNK)
    fw = final_norm_w.reshape(1, D_MODEL)
    fused_final = False
    w_in_t = jnp.swapaxes(w_in, 1, 2)
    tm = min(TM, xs.shape[0])
    for layer in range(DEPTH):
        h_mix = _norm_cast(xs, mix_norm_w[layer].reshape(1, D_MODEL), tm=tm)
        xs = _mixer(xs, h_mix, layer, w_in_t, lower_bound_logits, gk_up_w, gk_up_b, hgrn_norm_w,
                    gla_norm_w, w_proj_a, w_proj_b, w_out, consts)
        nw = ffn_norm_w[layer].reshape(1, D_MODEL)
        j = layer // 2
        if layer % 2 == 0:
            xs = _dense_ffn(xs, nw, dense_w_gate, dense_w_up, dense_w_down, j, tm=tm, tf=TF_DENSE)
        else:
            fused_final = layer == DEPTH - 1
            xs = _moe_block(xs, nw, router_w[j], router_b[j], moe_w_gate, moe_w_up, moe_w_down, j,
                            fw, fused_final)
    if not fused_final:
        xs = _final_norm(xs, fw, tm=min(TM, xs.shape[0]))
    return xs.reshape(b, t, d)
```

```python
import functools

import numpy as np
import jax
import jax.numpy as jnp
from jax import lax
from jax.experimental import pallas as pl
from jax.experimental.pallas import tpu as pltpu

F32 = jnp.float32
BF16 = jnp.bfloat16

D_MODEL = 2048
DEPTH = 2
HGRN_HEADS = 8
HGRN_KEY = 1024
HGRN_VAL = 1024
GLA_HEADS = 4
GLA_KEY = 512
GLA_VAL = 1024
HEAD_K = 128
GLA_GATE_RANK = 16
GLA_GATE_NORMALIZER = 16.0
D_FF_DENSE = 5632
N_EXPERTS = 8
TOP_K = 2
D_FF_EXPERT = 7168
EPS = 1e-6

MAIN_COLS = 4 * 1024 + 2 * 512 + 2 * 1024
GATE_COLS = 2 * D_MODEL

LANE = 128
VMEM_LIMIT = 56 * 1024 * 1024

TM = 1024
TN_IN = 512
TN_OUT = 512
TF_DENSE = 256
TF_MOE = 256
MOE_TM = 2304
MOE_SUB = 256
ATT_CHUNK = 128
ATT_TB = 1024
ROW_BATCH = 256
ROW_CHUNKS = 4
TM_PROJ = 2048


def _cparams(sem):
    return pltpu.CompilerParams(dimension_semantics=sem, vmem_limit_bytes=VMEM_LIMIT)


def _rms(x, w):
    return x * lax.rsqrt(jnp.mean(x * x, axis=-1, keepdims=True) + EPS) * w


def _sigmoid(x):
    return 0.5 * jnp.tanh(0.5 * x) + 0.5


def _log_sigmoid(x):
    return jnp.minimum(x, 0.0) - jnp.log1p(jnp.exp(-jnp.abs(x)))


def _bdot(a, b):
    return jnp.dot(a, b, preferred_element_type=F32)


_NT = (((1,), (1,)), ((), ()))
_TN = (((0,), (0,)), ((), ()))


def _bdot_nt(a, b):
    return lax.dot_general(a, b, _NT, preferred_element_type=F32)


def _row_chunks(tm):
    step = max(tm // ROW_CHUNKS, 8)
    return [slice(r, r + step) for r in range(0, tm, step)]


def _single_buffered(block_shape, index_map):
    return pl.BlockSpec(block_shape, index_map, pipeline_mode=pl.Buffered(1))


def _norm_cast_kernel(x_ref, w_ref, h_ref):
    h_ref[...] = _rms(x_ref[...], w_ref[...]).astype(BF16)


def _norm_cast(x, w, *, tm):
    t = x.shape[0]
    return pl.pallas_call(
        _norm_cast_kernel,
        grid=(t // tm,),
        in_specs=[pl.BlockSpec((tm, D_MODEL), lambda i: (i, 0)),
                  pl.BlockSpec((1, D_MODEL), lambda i: (0, 0))],
        out_specs=pl.BlockSpec((tm, D_MODEL), lambda i: (i, 0)),
        out_shape=jax.ShapeDtypeStruct((t, D_MODEL), BF16),
        compiler_params=_cparams(("parallel",)),
        name="norm_cast",
    )(x, w)


def _inproj_main_kernel(h_ref, w_ref, lb_ref, u_ref, logf_ref, *, tn):
    j = pl.program_id(1)
    chunks = _row_chunks(h_ref.shape[0])

    def project(rows):
        return _bdot_nt(h_ref[rows, :], w_ref[...].astype(BF16))

    per = 1024 // tn
    half = 512 // tn
    scale = HEAD_K ** -0.5
    b_f, b_i, b_g = per, 2 * per, 3 * per
    b_qb = 4 * per
    b_kb = b_qb + half
    b_vb = b_kb + half
    b_gb = b_vb + per

    is_q = (j < b_f) | ((j >= b_qb) & (j < b_kb))
    is_f = (j >= b_f) & (j < b_i)
    is_gate = ((j >= b_g) & (j < b_qb)) | (j >= b_gb)
    is_plain = jnp.logical_not(is_q | is_f | is_gate)

    @pl.when(is_q)
    def _():
        for rows in chunks:
            u_ref[rows, :] = (project(rows) * scale).astype(BF16)

    @pl.when(is_plain)
    def _():
        for rows in chunks:
            u_ref[rows, :] = project(rows).astype(BF16)

    @pl.when(is_gate)
    def _():
        for rows in chunks:
            acc = project(rows)
            u_ref[rows, :] = (acc * _sigmoid(acc)).astype(BF16)

    @pl.when(is_f)
    def _():
        lb = lb_ref[...]
        log_lb = jnp.log(lb)
        log_1m_lb = jnp.log1p(-lb)
        for rows in chunks:
            acc = project(rows)
            e = jnp.exp(-jnp.abs(acc))
            r = 1.0 / (1.0 + e)
            sig_neg = jnp.where(acc >= 0, e * r, r)
            ls = jnp.minimum(acc, 0.0) - jnp.log1p(e)
            b = log_1m_lb + ls
            m = jnp.maximum(log_lb, b)
            logf_ref[rows, :] = m + jnp.log1p(jnp.exp(-jnp.abs(log_lb - b)))
            u_ref[rows, :] = ((1.0 - lb) * sig_neg).astype(BF16)


def _inproj_main(h, w_in, layer, lb, *, tm, tn):
    t = h.shape[0]
    per = 1024 // tn
    nj = MAIN_COLS // tn
    kern = functools.partial(_inproj_main_kernel, tn=tn)
    return pl.pallas_call(
        kern,
        grid=(t // tm, nj),
        in_specs=[
            pl.BlockSpec((tm, D_MODEL), lambda i, j: (i, 0)),
            pl.BlockSpec((None, tn, D_MODEL), lambda i, j: (layer, j, 0)),
            pl.BlockSpec((1, tn), lambda i, j: (0, jnp.clip(j - per, 0, per - 1))),
        ],
        out_specs=[
            pl.BlockSpec((tm, tn), lambda i, j: (i, j)),
            pl.BlockSpec((tm, tn), lambda i, j: (i, jnp.clip(j - per, 0, per - 1))),
        ],
        out_shape=[
            jax.ShapeDtypeStruct((t, MAIN_COLS), BF16),
            jax.ShapeDtypeStruct((t, HGRN_KEY), F32),
        ],
        compiler_params=_cparams(("parallel", "arbitrary")),
        name="inproj_main",
    )(h, w_in, lb)


def _inproj_tail_kernel(h_ref, wm_ref, wgk_ref, up_ref, upb_ref, gates_ref, loggk_ref):
    j = pl.program_id(1)
    chunks = _row_chunks(h_ref.shape[0])

    @pl.when(j == 0)
    def _():
        wgk = wgk_ref[0].astype(BF16)
        up = up_ref[...].astype(BF16)
        for rows in chunks:
            code = _bdot_nt(h_ref[rows, :], wgk)
            lg = _bdot(code.astype(BF16), up) + upb_ref[...]
            loggk_ref[rows, :] = _log_sigmoid(lg) * (1.0 / GLA_GATE_NORMALIZER)

    for rows in chunks:
        acc = _bdot_nt(h_ref[rows, :], wm_ref[0].astype(BF16))
        gates_ref[rows, :] = _sigmoid(acc).astype(BF16)


def _inproj_tail(h, w_in, layer, up_w, up_b, *, tm, tn):
    t = h.shape[0]
    gate_row0 = MAIN_COLS + GLA_GATE_RANK
    return pl.pallas_call(
        _inproj_tail_kernel,
        grid=(t // tm, GATE_COLS // tn),
        in_specs=[
            pl.BlockSpec((tm, D_MODEL), lambda i, j: (i, 0)),
            pl.BlockSpec((pl.Element(1), pl.Element(tn), pl.Element(D_MODEL)),
                         lambda i, j: (layer, pl.multiple_of(gate_row0 + j * tn, 8), 0)),
            pl.BlockSpec((pl.Element(1), pl.Element(GLA_GATE_RANK), pl.Element(D_MODEL)),
                         lambda i, j: (layer, MAIN_COLS, 0)),
            pl.BlockSpec((GLA_GATE_RANK, GLA_KEY), lambda i, j: (0, 0)),
            pl.BlockSpec((1, GLA_KEY), lambda i, j: (0, 0)),
        ],
        out_specs=[
            pl.BlockSpec((tm, tn), lambda i, j: (i, j)),
            pl.BlockSpec((tm, GLA_KEY), lambda i, j: (i, 0)),
        ],
        out_shape=[
            jax.ShapeDtypeStruct((t, GATE_COLS), BF16),
            jax.ShapeDtypeStruct((t, GLA_KEY), F32),
        ],
        compiler_params=_cparams(("parallel", "arbitrary")),
        name="inproj_tail",
    )(h, w_in, w_in, up_w, up_b)


def _recurrence_constants(c):
    levels = int(np.log2(c))
    t = np.arange(c)[:, None]
    s = np.arange(c)[None, :]
    mats = [(s <= t), (s > t)]
    masks = [np.eye(c, dtype=bool)]
    for l in range(levels):
        b = 1 << l
        tb = (t // (2 * b)) * 2 * b + b - 1
        is_i = (t // b) % 2 == 1
        mats.append(np.where(is_i, (s > tb) & (s <= t), (s > t) & (s <= tb)))
        i = np.arange(c)[:, None]
        j = np.arange(c)[None, :]
        masks.append((i // (2 * b) == j // (2 * b)) & ((i // b) % 2 == 1) & ((j // b) % 2 == 0))
    m = np.concatenate(mats, axis=0).astype(np.float32)
    m2 = np.concatenate([m, m], axis=1)
    masks2 = np.tile(np.stack(masks).astype(np.float32), (1, 1, 2))
    return jnp.asarray(m2, BF16), jnp.asarray(masks2)


def _recurrence_kernel(q_ref, k_ref, g_ref, v_ref, sg_ref, nw_ref, m2_ref, masks_ref, y_ref, st_ref,
                       *, c, dk, dv, nchunk, npairs):
    levels = int(np.log2(c))
    w = 2 * dk

    @pl.when(pl.program_id(1) == 0)
    def _():
        st_ref[...] = jnp.zeros_like(st_ref)

    row = lax.broadcasted_iota(jnp.int32, (c, w), 0)
    shift = int(np.log2(dk))
    same_head = ((lax.broadcasted_iota(jnp.int32, (w, w), 0) >> shift)
                 == (lax.broadcasted_iota(jnp.int32, (w, w), 1) >> shift))
    nw = nw_ref[...]

    def block_diag(b):
        return jnp.where(same_head, jnp.concatenate([b, b], axis=0), jnp.zeros((), b.dtype))

    def pair_nt(a, b):
        return lax.dot_general(a, block_diag(b), _NT, preferred_element_type=F32)

    def chunk_body(ci, carry):
        r0 = pl.multiple_of(ci * c, c)
        rows = pl.ds(r0, c)
        for pp in range(npairs):
            pair_chunk(rows, pp)
        return carry

    def pair_chunk(rows, pp):
        kcols = slice(pp * w, (pp + 1) * w)
        vbase = pp * 2 * dv
        g = g_ref[rows, kcols]
        g_hi = g.astype(BF16)
        g_lo = (g - g_hi.astype(F32)).astype(BF16)
        d_all = _bdot(m2_ref[...], jnp.concatenate([g_hi, g_lo], axis=0))
        e_all = jnp.exp(d_all)
        q = q_ref[rows, kcols]
        k = k_ref[rows, kcols]
        qf = q.astype(F32)
        kf = k.astype(F32)
        e_cum = e_all[0:c]
        q_in = (qf * e_cum).astype(BF16)
        k_out = (kf * e_all[c:2 * c]).astype(BF16)
        scores = pair_nt(q, k) * masks_ref[0]
        for l in range(levels):
            b = 1 << l
            if b >= 8:
                parts = [(qf if (r // b) % 2 == 1 else kf)[r:r + b] for r in range(0, c, b)]
                qk = jnp.concatenate(parts, axis=0)
            else:
                qk = jnp.where(((row >> l) & 1) == 1, qf, kf)
            x = (qk * e_all[(l + 2) * c:(l + 3) * c]).astype(BF16)
            scores = scores + pair_nt(x, x) * masks_ref[l + 1]
        sb = scores.astype(BF16)
        v = v_ref[rows, vbase:vbase + 2 * dv]
        decay = e_cum[c - 1:c, :]
        if dv == dk:
            st = st_ref[pp]
            o = _bdot(sb, block_diag(v)) + lax.dot_general(
                q_in, st.astype(BF16), _NT, preferred_element_type=F32)
            upd = lax.dot_general(v, k_out, _TN, preferred_element_type=F32)
            st_ref[pp] = st * decay + jnp.where(same_head, upd, 0.0)
            outs = [o[:, :dv], o[:, dv:]]
        else:
            outs = []
            for hh in range(2):
                ks = slice(hh * dk, (hh + 1) * dk)
                vh = v[:, hh * dv:(hh + 1) * dv]
                st = st_ref[2 * pp + hh]
                outs.append(_bdot(sb[:, ks], vh) + lax.dot_general(
                    q_in[:, ks], st.astype(BF16), _NT, preferred_element_type=F32))
                st_ref[2 * pp + hh] = st * decay[:, ks] + lax.dot_general(
                    vh, k_out[:, ks], _TN, preferred_element_type=F32)
        for hh, o in enumerate(outs):
            vs = slice(vbase + hh * dv, vbase + (hh + 1) * dv)
            o = o * lax.rsqrt(jnp.mean(o * o, axis=-1, keepdims=True) + EPS) * nw
            y_ref[rows, vs] = (o * sg_ref[rows, vs].astype(F32)).astype(BF16)

    lax.fori_loop(0, nchunk, chunk_body, 0)


def _recurrence(q_arr, q_off, k_arr, k_off, g_arr, v_arr, v_off, sg_arr, sg_off, nw, consts,
                *, heads, dv, hb, tb, c):
    t = q_arr.shape[0]
    dk = HEAD_K
    assert hb % 2 == 0 and c == dk
    npairs = hb // 2
    m2, masks = consts
    kern = functools.partial(_recurrence_kernel, c=c, dk=dk, dv=dv, nchunk=tb // c, npairs=npairs)
    wk, wv = hb * dk, hb * dv
    st_shape = (npairs, 2 * dv, 2 * dk) if dv == dk else (hb, dv, dk)
    return pl.pallas_call(
        kern,
        grid=(heads // hb, t // tb),
        in_specs=[
            pl.BlockSpec((tb, wk), lambda h, i: (i, q_off + h)),
            pl.BlockSpec((tb, wk), lambda h, i: (i, k_off + h)),
            pl.BlockSpec((tb, wk), lambda h, i: (i, h)),
            pl.BlockSpec((tb, wv), lambda h, i: (i, v_off + h)),
            pl.BlockSpec((tb, wv), lambda h, i: (i, sg_off + h)),
            pl.BlockSpec((1, dv), lambda h, i: (0, 0)),
            pl.BlockSpec(m2.shape, lambda h, i: (0, 0)),
            pl.BlockSpec(masks.shape, lambda h, i: (0, 0, 0)),
        ],
        out_specs=pl.BlockSpec((tb, wv), lambda h, i: (i, h)),
        out_shape=jax.ShapeDtypeStruct((t, heads * dv), BF16),
        scratch_shapes=[pltpu.VMEM(st_shape, F32)],
        compiler_params=_cparams(("parallel", "arbitrary")),
        name="recurrence_dv%d" % dv,
    )(q_arr, k_arr, g_arr, v_arr, sg_arr, nw, m2, masks)


def _merge_kernel(ya_ref, yb_ref, wa_ref, wb_ref, ga_ref, gb_ref, o_ref):
    pa = _bdot(ya_ref[...], wa_ref[...].astype(BF16))
    pb = _bdot(yb_ref[...], wb_ref[...].astype(BF16))
    o_ref[...] = (ga_ref[...].astype(F32) * pa + gb_ref[...].astype(F32) * pb).astype(BF16)


def _merge(ya, yb, wa, wb, layer, gates, *, tm, tn):
    t = ya.shape[0]
    nj = D_MODEL // tn
    return pl.pallas_call(
        _merge_kernel,
        grid=(t // tm, nj),
        in_specs=[
            pl.BlockSpec((tm, HGRN_VAL), lambda i, j: (i, 0)),
            pl.BlockSpec((tm, GLA_VAL), lambda i, j: (i, 0)),
            pl.BlockSpec((None, HGRN_VAL, tn), lambda i, j: (layer, 0, j)),
            pl.BlockSpec((None, GLA_VAL, tn), lambda i, j: (layer, 0, j)),
            pl.BlockSpec((tm, tn), lambda i, j: (i, j)),
            pl.BlockSpec((tm, tn), lambda i, j: (i, nj + j)),
        ],
        out_specs=pl.BlockSpec((tm, tn), lambda i, j: (i, j)),
        out_shape=jax.ShapeDtypeStruct((t, D_MODEL), BF16),
        compiler_params=_cparams(("parallel", "arbitrary")),
        name="merge",
    )(ya, yb, wa, wb, gates, gates)


def _outproj_kernel(m_ref, w_ref, x_ref, o_ref):
    o_ref[...] = x_ref[...] + _bdot(m_ref[...], w_ref[...].astype(BF16))


def _outproj(merged, w_out, layer, x, *, tm, tn):
    t = x.shape[0]
    return pl.pallas_call(
        _outproj_kernel,
        grid=(t // tm, D_MODEL // tn),
        in_specs=[
            pl.BlockSpec((tm, D_MODEL), lambda i, j: (i, 0)),
            pl.BlockSpec((None, D_MODEL, tn), lambda i, j: (layer, 0, j)),
            pl.BlockSpec((tm, tn), lambda i, j: (i, j)),
        ],
        out_specs=pl.BlockSpec((tm, tn), lambda i, j: (i, j)),
        out_shape=jax.ShapeDtypeStruct((t, D_MODEL), F32),
        compiler_params=_cparams(("parallel", "arbitrary")),
        name="outproj",
    )(merged, w_out, x)


def _dense_ffn_kernel(x_ref, nw_ref, wg_ref, wu_ref, wd_ref, o_ref, h_scr):
    j = pl.program_id(1)

    @pl.when(j == 0)
    def _():
        x = x_ref[...]
        h_scr[...] = _rms(x, nw_ref[...]).astype(BF16)
        o_ref[...] = x

    h = h_scr[...]
    g = _bdot(h, wg_ref[...].astype(BF16))
    u = _bdot(h, wu_ref[...].astype(BF16))
    a = (g * _sigmoid(g) * u).astype(BF16)
    o_ref[...] += _bdot(a, wd_ref[...].astype(BF16))


def _dense_ffn(x, nw, wg, wu, wd, idx, *, tm, tf):
    t = x.shape[0]
    return pl.pallas_call(
        _dense_ffn_kernel,
        grid=(t // tm, D_FF_DENSE // tf),
        in_specs=[
            pl.BlockSpec((tm, D_MODEL), lambda i, j: (i, 0)),
            pl.BlockSpec((1, D_MODEL), lambda i, j: (0, 0)),
            pl.BlockSpec((None, D_MODEL, tf), lambda i, j: (idx, 0, j)),
            pl.BlockSpec((None, D_MODEL, tf), lambda i, j: (idx, 0, j)),
            pl.BlockSpec((None, tf, D_MODEL), lambda i, j: (idx, j, 0)),
        ],
        out_specs=pl.BlockSpec((tm, D_MODEL), lambda i, j: (i, 0)),
        out_shape=jax.ShapeDtypeStruct((t, D_MODEL), F32),
        scratch_shapes=[pltpu.VMEM((tm, D_MODEL), BF16)],
        compiler_params=_cparams(("parallel", "arbitrary")),
        name="dense_ffn",
    )(x, nw, wg, wu, wd)


def _router_kernel(x_ref, nw_ref, rw_ref, rb_ref, h_ref, idx_ref, wts_ref):
    h = _rms(x_ref[...], nw_ref[...])
    h_ref[...] = h
    logits = jnp.dot(h, rw_ref[...], preferred_element_type=F32,
                     precision=lax.Precision.HIGHEST) + rb_ref[...]
    lane_i = lax.broadcasted_iota(jnp.int32, logits.shape, 1)
    lane = lane_i.astype(F32)
    m1 = jnp.max(logits, axis=-1, keepdims=True)
    i1 = jnp.min(jnp.where(logits == m1, lane, float(LANE)), axis=-1, keepdims=True)
    rest = jnp.where(lane == i1, -jnp.inf, logits)
    m2 = jnp.max(rest, axis=-1, keepdims=True)
    i2 = jnp.min(jnp.where(rest == m2, lane, float(LANE)), axis=-1, keepdims=True)
    e = jnp.exp(m2 - m1)
    w1 = 1.0 / (1.0 + e)
    w2 = e * w1
    idx_ref[...] = jnp.where(lane_i == 0, i1, jnp.where(lane_i == 1, i2, 0.0)).astype(jnp.int32)
    wts_ref[...] = jnp.where(lane_i == 0, w1, jnp.where(lane_i == 1, w2, 0.0))


def _router(x, nw, rw_pad, rb_pad, *, tm):
    t = x.shape[0]
    return pl.pallas_call(
        _router_kernel,
        grid=(t // tm,),
        in_specs=[
            pl.BlockSpec((tm, D_MODEL), lambda i: (i, 0)),
            pl.BlockSpec((1, D_MODEL), lambda i: (0, 0)),
            pl.BlockSpec((D_MODEL, LANE), lambda i: (0, 0)),
            pl.BlockSpec((1, LANE), lambda i: (0, 0)),
        ],
        out_specs=[
            pl.BlockSpec((tm, D_MODEL), lambda i: (i, 0)),
            pl.BlockSpec((tm, LANE), lambda i: (i, 0)),
            pl.BlockSpec((tm, LANE), lambda i: (i, 0)),
        ],
        out_shape=[
            jax.ShapeDtypeStruct((t, D_MODEL), F32),
            jax.ShapeDtypeStruct((t, LANE), jnp.int32),
            jax.ShapeDtypeStruct((t, LANE), F32),
        ],
        compiler_params=_cparams(("parallel",)),
        name="router",
    )(x, nw, rw_pad, rb_pad)


def _dispatch_kernel(dest_ref, pad_start_ref, pad_count_ref, h_ref, xs_hbm, zero_scr, sem, *, nb, n_tok):
    step = pl.program_id(0)
    n_tok_steps = n_tok // nb

    def row_copy(src_ref, src_row, dst_row):
        return pltpu.make_async_copy(src_ref.at[pl.ds(src_row, 1), :],
                                     xs_hbm.at[pl.ds(dst_row, 1), :], sem.at[0])

    @pl.when(step < n_tok_steps)
    def _():
        base = step * (nb * TOP_K)

        def issue(r, c):
            for kk in range(TOP_K):
                row_copy(h_ref, r, dest_ref[base + r * TOP_K + kk]).start()
            return c

        lax.fori_loop(0, nb, issue, 0, unroll=4)

        for kk in range(TOP_K):
            pltpu.make_async_copy(h_ref, xs_hbm.at[pl.ds(0, nb), :], sem.at[0]).wait()

    @pl.when(step >= n_tok_steps)
    def _():
        e = step - n_tok_steps
        zero_scr[...] = jnp.zeros_like(zero_scr)
        start = pad_start_ref[e]
        n = pad_count_ref[e]

        def issue(r, c):
            row_copy(zero_scr, 0, start + r).start()
            return c

        lax.fori_loop(0, n, issue, 0)

        def drain(r, c):
            row_copy(zero_scr, 0, 0).wait()
            return c

        lax.fori_loop(0, n, drain, 0)


def _dispatch(dest, pad_start, pad_count, h, t_pad, *, nb):
    n_tok = h.shape[0]
    n_tok_steps = n_tok // nb
    kern = functools.partial(_dispatch_kernel, nb=nb, n_tok=n_tok)
    grid_spec = pltpu.PrefetchScalarGridSpec(
        num_scalar_prefetch=3,
        grid=(n_tok_steps + N_EXPERTS,),
        in_specs=[pl.BlockSpec((nb, D_MODEL),
                               lambda i, d, ps, pc: (jnp.minimum(i, n_tok_steps - 1), 0))],
        out_specs=pl.BlockSpec(memory_space=pl.ANY),
        scratch_shapes=[pltpu.VMEM((8, D_MODEL), F32), pltpu.SemaphoreType.DMA((1,))],
    )
    return pl.pallas_call(
        kern,
        grid_spec=grid_spec,
        out_shape=jax.ShapeDtypeStruct((t_pad, D_MODEL), F32),
        compiler_params=_cparams(("arbitrary",)),
        name="dispatch",
    )(dest, pad_start, pad_count, h)


def _moe_ffn_kernel(te_ref, tv_ref, tb_ref, xs_hbm, wg_ref, wu_ref, wd_ref, ys_hbm,
                    x_scr, acc_scr, wg_scr, wu_scr, wd_scr, a_scr, pend_ref, sem_in, sem_out, *, sub, tm):
    i = pl.program_id(0)
    j = pl.program_id(1)
    last_i = pl.num_programs(0) - 1
    last_j = pl.num_programs(1) - 1
    valid = tv_ref[i]
    nsub = (valid + sub - 1) // sub
    tile_row0 = tb_ref[i] * tm

    def in_copy(s):
        r = pl.multiple_of(s * sub, sub)
        return pltpu.make_async_copy(xs_hbm.at[pl.ds(tile_row0 + r, sub), :],
                                     x_scr.at[pl.ds(r, sub), :], sem_in.at[0])

    def out_copy(s):
        r = pl.multiple_of(s * sub, sub)
        return pltpu.make_async_copy(acc_scr.at[pl.ds(r, sub), :],
                                     ys_hbm.at[pl.ds(tile_row0 + r, sub), :], sem_out.at[0])

    def wait_pending_out():
        def body(s, c):
            out_copy(0).wait()
            return c

        lax.fori_loop(0, pend_ref[0], body, 0)
        pend_ref[0] = 0

    @pl.when((i == 0) & (j == 0))
    def _():
        pend_ref[0] = 0

    @pl.when(valid > 0)
    def _():
        @pl.when(j == 0)
        def _():
            wait_pending_out()

            def start_in(s, c):
                in_copy(s).start()
                return c

            lax.fori_loop(0, nsub, start_in, 0)

            def zero_rows(s, c):
                acc_scr[pl.ds(pl.multiple_of(s * sub, sub), sub), :] = jnp.zeros((sub, D_MODEL), F32)
                return c

            lax.fori_loop(0, nsub, zero_rows, 0)

            def wait_in(s, c):
                in_copy(0).wait()
                return c

            lax.fori_loop(0, nsub, wait_in, 0)

        def cast_weights():
            wg_scr[...] = wg_ref[...].astype(BF16)
            wu_scr[...] = wu_ref[...].astype(BF16)
            wd_scr[...] = wd_ref[...].astype(BF16)

        def up(row0, nrows):
            h = x_scr[pl.ds(row0, nrows), :].astype(BF16)
            g = _bdot(h, wg_scr[...])
            u = _bdot(h, wu_scr[...])
            return (g * _sigmoid(g) * u).astype(BF16)

        def down(row0, nrows, a):
            acc_scr[pl.ds(row0, nrows), :] += _bdot(a, wd_scr[...])

        npair = nsub // 2
        blk = 2 * sub

        @pl.when(npair == 0)
        def _():
            cast_weights()

        @pl.when(npair > 0)
        def _():
            cast_weights()
            a_scr[...] = up(0, blk)

            def pair_body(p, c):
                base = pl.multiple_of(p * blk, blk)
                prev = a_scr[...]
                nxt = up(base, blk)
                down(base - blk, blk, prev)
                a_scr[...] = nxt
                return c

            lax.fori_loop(1, npair, pair_body, 0)
            down(pl.multiple_of((npair - 1) * blk, blk), blk, a_scr[...])

        @pl.when(nsub % 2 == 1)
        def _():
            last = pl.multiple_of((nsub - 1) * sub, sub)
            down(last, sub, up(last, sub))

        @pl.when(j == last_j)
        def _():
            def start_out(s, c):
                out_copy(s).start()
                return c

            lax.fori_loop(0, nsub, start_out, 0)
            pend_ref[0] = nsub

    @pl.when((i == last_i) & (j == last_j))
    def _():
        wait_pending_out()


def _moe_ffn(tile_expert, tile_valid, tile_blk, xs, wg, wu, wd, idx, *, tm, tf, sub):
    t_pad = xs.shape[0]
    n_tiles = t_pad // tm
    nf = D_FF_EXPERT // tf
    kern = functools.partial(_moe_ffn_kernel, sub=sub, tm=tm)

    def jj(i, j, tv):
        return jnp.where(tv[i] > 0, j, nf - 1)

    grid_spec = pltpu.PrefetchScalarGridSpec(
        num_scalar_prefetch=3,
        grid=(n_tiles, nf),
        in_specs=[
            pl.BlockSpec(memory_space=pl.ANY),
            pl.BlockSpec((None, None, D_MODEL, tf), lambda i, j, te, tv, tb: (idx, te[i], 0, jj(i, j, tv))),
            pl.BlockSpec((None, None, D_MODEL, tf), lambda i, j, te, tv, tb: (idx, te[i], 0, jj(i, j, tv))),
            pl.BlockSpec((None, None, tf, D_MODEL), lambda i, j, te, tv, tb: (idx, te[i], jj(i, j, tv), 0)),
        ],
        out_specs=pl.BlockSpec(memory_space=pl.ANY),
        scratch_shapes=[
            pltpu.VMEM((tm, D_MODEL), F32),
            pltpu.VMEM((tm, D_MODEL), F32),
            pltpu.VMEM((D_MODEL, tf), BF16),
            pltpu.VMEM((D_MODEL, tf), BF16),
            pltpu.VMEM((tf, D_MODEL), BF16),
            pltpu.VMEM((2 * sub, tf), BF16),
            pltpu.SMEM((1,), jnp.int32),
            pltpu.SemaphoreType.DMA((1,)),
            pltpu.SemaphoreType.DMA((1,)),
        ],
    )
    return pl.pallas_call(
        kern,
        grid_spec=grid_spec,
        out_shape=jax.ShapeDtypeStruct((t_pad, D_MODEL), F32),
        compiler_params=_cparams(("arbitrary", "arbitrary")),
        name="moe_ffn",
    )(tile_expert, tile_valid, tile_blk, xs, wg, wu, wd)


def _combine_kernel(dest_ref, x_ref, wts_ref, fw_ref, ys_hbm, o_ref, ya_scr, yb_scr, sem,
                    *, nb, final_norm):
    step = pl.program_id(0)
    nsteps = pl.num_programs(0)
    slot = step % 2

    def row_copy(src_row, dst_ref, slot_, r):
        return pltpu.make_async_copy(ys_hbm.at[pl.ds(src_row, 1), :],
                                     dst_ref.at[slot_, pl.ds(r, 1), :], sem.at[slot_])

    def gather(block, slot_):
        def issue(r, c):
            s = (block * nb + r) * TOP_K
            row_copy(dest_ref[s], ya_scr, slot_, r).start()
            row_copy(dest_ref[s + 1], yb_scr, slot_, r).start()
            return c

        lax.fori_loop(0, nb, issue, 0, unroll=4)

    @pl.when(step == 0)
    def _():
        gather(0, 0)

    @pl.when(step + 1 < nsteps)
    def _():
        gather(step + 1, 1 - slot)

    for buf in (ya_scr, yb_scr):
        pltpu.make_async_copy(ys_hbm.at[pl.ds(0, nb), :], buf.at[slot], sem.at[slot]).wait()

    w = wts_ref[...]
    out = x_ref[...] + w[:, 0:1] * ya_scr[slot] + w[:, 1:2] * yb_scr[slot]
    if final_norm:
        out = _rms(out, fw_ref[...])
    o_ref[...] = out


def _combine(dest, x, wts, fw, ys, *, nb, final_norm):
    t = x.shape[0]
    kern = functools.partial(_combine_kernel, nb=nb, final_norm=final_norm)
    grid_spec = pltpu.PrefetchScalarGridSpec(
        num_scalar_prefetch=1,
        grid=(t // nb,),
        in_specs=[
            pl.BlockSpec((nb, D_MODEL), lambda i, d: (i, 0)),
            pl.BlockSpec((nb, LANE), lambda i, d: (i, 0)),
            pl.BlockSpec((1, D_MODEL), lambda i, d: (0, 0)),
            pl.BlockSpec(memory_space=pl.ANY),
        ],
        out_specs=pl.BlockSpec((nb, D_MODEL), lambda i, d: (i, 0)),
        scratch_shapes=[
            pltpu.VMEM((2, nb, D_MODEL), F32),
            pltpu.VMEM((2, nb, D_MODEL), F32),
            pltpu.SemaphoreType.DMA((2,)),
        ],
    )
    return pl.pallas_call(
        kern,
        grid_spec=grid_spec,
        out_shape=jax.ShapeDtypeStruct((t, D_MODEL), F32),
        compiler_params=_cparams(("arbitrary",)),
        name="combine",
    )(dest, x, wts, fw, ys)


def _routing_tables(idx2, tm, sub):
    n_slots = idx2.shape[0] * TOP_K
    e_flat = idx2.reshape(-1)
    onehot = (e_flat[:, None] == jnp.arange(N_EXPERTS, dtype=jnp.int32)[None, :]).astype(jnp.int32)
    csum = jnp.cumsum(onehot, axis=0)
    counts = csum[-1]
    rank = jnp.sum(onehot * csum, axis=1) - 1
    padded = ((counts + tm - 1) // tm) * tm
    gend = jnp.cumsum(padded)
    gstart = gend - padded
    dest = (jnp.sum(onehot * gstart[None, :], axis=1) + rank).astype(jnp.int32)

    n_tiles = n_slots // tm + N_EXPERTS
    t_pad = n_tiles * tm
    tile_row0 = jnp.arange(n_tiles, dtype=jnp.int32) * tm
    n_used = gend[-1] // tm
    tile_blk = jnp.minimum(jnp.arange(n_tiles, dtype=jnp.int32), n_used - 1)
    blk_row0 = tile_blk * tm
    tile_expert = jnp.sum((blk_row0[:, None] >= gend[None, :]).astype(jnp.int32), axis=1)
    tile_expert = jnp.minimum(tile_expert, N_EXPERTS - 1)
    valid = jnp.clip(gstart[tile_expert] + counts[tile_expert] - tile_row0, 0, tm)
    tile_valid = jnp.where(jnp.arange(n_tiles) < n_used, valid, 0).astype(jnp.int32)

    pad_start = (gstart + counts).astype(jnp.int32)
    pad_count = (((counts + sub - 1) // sub) * sub - counts).astype(jnp.int32)
    return (dest, pad_start, pad_count, tile_expert.astype(jnp.int32), tile_valid,
            tile_blk.astype(jnp.int32), t_pad)


def _moe_block(x, nw, rw, rb, wg, wu, wd, idx, fw, final_norm):
    t = x.shape[0]
    tm = min(TM, t)
    rw_pad = jnp.pad(rw, ((0, 0), (0, LANE - N_EXPERTS)))
    rb_pad = jnp.pad(rb, (0, LANE - N_EXPERTS), constant_values=-1e30).reshape(1, LANE)
    moe_tm = min(MOE_TM, TOP_K * t)
    h, idx_l, wts = _router(x, nw, rw_pad, rb_pad, tm=tm)
    dest, pad_start, pad_count, tile_expert, tile_valid, tile_blk, t_pad = _routing_tables(
        idx_l[:, :TOP_K], moe_tm, MOE_SUB)
    xs = _dispatch(dest, pad_start, pad_count, h, t_pad, nb=ROW_BATCH)
    ys = _moe_ffn(tile_expert, tile_valid, tile_blk, xs, wg, wu, wd, idx, tm=moe_tm, tf=TF_MOE,
                  sub=MOE_SUB)
    return _combine(dest, x, wts, fw, ys, nb=ROW_BATCH, final_norm=final_norm)


def _final_norm_kernel(x_ref, w_ref, o_ref):
    o_ref[...] = _rms(x_ref[...], w_ref[...])


def _final_norm(x, w, *, tm):
    t = x.shape[0]
    return pl.pallas_call(
        _final_norm_kernel,
        grid=(t // tm,),
        in_specs=[pl.BlockSpec((tm, D_MODEL), lambda i: (i, 0)),
                  pl.BlockSpec((1, D_MODEL), lambda i: (0, 0))],
        out_specs=pl.BlockSpec((tm, D_MODEL), lambda i: (i, 0)),
        out_shape=jax.ShapeDtypeStruct((t, D_MODEL), F32),
        compiler_params=_cparams(("parallel",)),
        name="final_norm",
    )(x, w)


def _mixer(x, h, layer, w_in, lower_bound_logits, gk_up_w, gk_up_b, hgrn_norm_w, gla_norm_w,
           w_proj_a, w_proj_b, w_out, consts):
    t = x.shape[0]
    tm = min(TM_PROJ, t)
    tb = min(ATT_TB, t)
    bounds = jnp.cumsum(jax.nn.softmax(lower_bound_logits.astype(F32), axis=0), axis=0)
    lb = (bounds[layer] - bounds[0]).reshape(1, HGRN_KEY)
    u, logf = _inproj_main(h, w_in, layer, lb, tm=tm, tn=TN_IN)
    gates, loggk = _inproj_tail(h, w_in, layer, gk_up_w[layer], gk_up_b[layer].reshape(1, GLA_KEY),
                                tm=tm, tn=TN_IN)
    hb_a, hb_b = 8, 4
    wa = hb_a * HEAD_K
    y_a = _recurrence(u, 0, u, 1024 // wa, logf, u, 2048 // wa, u, 3072 // wa,
                      hgrn_norm_w[layer].reshape(1, -1), consts,
                      heads=HGRN_HEADS, dv=128, hb=hb_a, tb=tb, c=ATT_CHUNK)
    wkb, wvb = hb_b * HEAD_K, hb_b * 256
    y_b = _recurrence(u, 4096 // wkb, u, 4608 // wkb, loggk, u, 5120 // wvb, u, 6144 // wvb,
                      gla_norm_w[layer].reshape(1, -1), consts,
                      heads=GLA_HEADS, dv=256, hb=hb_b, tb=tb, c=ATT_CHUNK)
    merged = _merge(y_a, y_b, w_proj_a, w_proj_b, layer, gates, tm=tm, tn=TN_OUT)
    return _outproj(merged, w_out, layer, x, tm=tm, tn=TN_OUT)


def kernel(x, mix_norm_w, w_in, lower_bound_logits, gk_up_w, gk_up_b, hgrn_norm_w, gla_norm_w, w_proj_a,
           w_proj_b, w_out, ffn_norm_w, dense_w_gate, dense_w_up, dense_w_down, router_w, router_b,
           moe_w_gate, moe_w_up, moe_w_down, final_norm_w):
    b, t, d = x.shape
    xs = x.reshape(b * t, d)
    consts = _recurrence_constants(ATT_CHUNK)
    fw = final_norm_w.reshape(1, D_MODEL)
    fused_final = False
    w_in_t = jnp.swapaxes(w_in, 1, 2)
    tm = min(TM, xs.shape[0])
    for layer in range(DEPTH):
        h_mix = _norm_cast(xs, mix_norm_w[layer].reshape(1, D_MODEL), tm=tm)
        xs = _mixer(xs, h_mix, layer, w_in_t, lower_bound_logits, gk_up_w, gk_up_b, hgrn_norm_w,
                    gla_norm_w, w_proj_a, w_proj_b, w_out, consts)
        nw = ffn_norm_w[layer].reshape(1, D_MODEL)
        j = layer // 2
        if layer % 2 == 0:
            xs = _dense_ffn(xs, nw, dense_w_gate, dense_w_up, dense_w_down, j, tm=tm, tf=TF_DENSE)
        else:
            fused_final = layer == DEPTH - 1
            xs = _moe_block(xs, nw, router_w[j], router_b[j], moe_w_gate, moe_w_up, moe_w_down, j,
                            fw, fused_final)
    if not fused_final:
        xs = _final_norm(xs, fw, tm=min(TM, xs.shape[0]))
    return xs.reshape(b, t, d)
```
